```python
import jax
import jax.numpy as jnp
from jax import lax
import numpy as np

D_MODEL = 2048
BATCH = 4
SEQ = 8192
DEPTH = 4

HEAD_DIM = 128
N_MIX_HEADS = D_MODEL // HEAD_DIM
HGRN_HEADS = N_MIX_HEADS // 4
GDN_HEADS = (N_MIX_HEADS - HGRN_HEADS) // 2
FOX_HEADS = N_MIX_HEADS - HGRN_HEADS - GDN_HEADS
GDN_W = GDN_HEADS * HEAD_DIM
FOX_W = FOX_HEADS * HEAD_DIM
HGRN_W = HGRN_HEADS * HEAD_DIM
CONV_K = 4
GDN_CHUNK = 64
HGRN_CHUNK = 16
FOX_BLOCK = 128
MEM_LEN = 256
XA_HEADS = 4
XA_HEAD_DIM = D_MODEL // XA_HEADS
D_FF = 5632
N_EXPERTS = 8
TOP_K = 2
EXPERT_FF = D_FF // 2
N_DENSE = (DEPTH + 1) // 2
N_MOE = DEPTH // 2
DEEPNORM_ALPHA = (2.0 * DEPTH) ** 0.25
DEEPNORM_BETA = (8.0 * DEPTH) ** -0.25
LN_EPS = 1e-5
RMS_EPS = 1e-6
MASK_VALUE = -1e30

IN_SPLITS = (GDN_W, GDN_W, GDN_W, GDN_W, GDN_HEADS, GDN_HEADS,
             FOX_W, FOX_W, FOX_W, FOX_HEADS,
             HGRN_W, HGRN_W, HGRN_W, HGRN_W)
IN_COLS = sum(IN_SPLITS)
IN_OFFSETS = tuple(sum(IN_SPLITS[:i + 1]) for i in range(len(IN_SPLITS) - 1))
VALUE_SLOTS = (2, 8, 12)

kernel_name = 'hybrid_gdn_fox_hgrn2_deepnorm_moe_trunk'


def _layer_norm(x, g, b):
    xf = x.astype(jnp.float32)
    mu = jnp.mean(xf, -1, keepdims=True)
    var = jnp.mean(jnp.square(xf - mu), -1, keepdims=True)
    y = (xf - mu) * lax.rsqrt(var + LN_EPS) * g.astype(jnp.float32) + b.astype(jnp.float32)
    return y.astype(x.dtype)


def _rms_norm(x, g):
    xf = x.astype(jnp.float32)
    return xf * lax.rsqrt(jnp.mean(jnp.square(xf), -1, keepdims=True) + RMS_EPS) * g.astype(jnp.float32)


def _l2_normalize(x):
    xf = x.astype(jnp.float32)
    return xf * lax.rsqrt(jnp.sum(jnp.square(xf), -1, keepdims=True) + 1e-6)


def _causal_depthwise_conv(x, w):
    return lax.conv_general_dilated(x, w[:, None, :], (1,), ((w.shape[0] - 1, 0),),
                                    dimension_numbers=('NWC', 'WIO', 'NWC'),
                                    feature_group_count=x.shape[-1])


def _heads(t, h):
    return t.reshape(t.shape[0], t.shape[1], h, HEAD_DIM)


def _to_chunks(t, c):
    b, s = t.shape[:2]
    return jnp.swapaxes(t.reshape((b, s // c, c) + t.shape[2:]), 2, 3)


def _from_chunks(t):
    b, n, h, c, d = t.shape
    return jnp.swapaxes(t, 2, 3).reshape(b, n * c, h, d)


def _gated_delta_rule(q, k, v, log_a, beta):
    f32 = jnp.float32
    c = GDN_CHUNK
    dk, dv = q.shape[-1], v.shape[-1]
    q = _to_chunks(q.astype(f32) * dk ** -0.5, c)
    k = _to_chunks(k.astype(f32), c)
    v = _to_chunks(v.astype(f32), c)
    g = jnp.cumsum(_to_chunks(log_a.astype(f32), c), axis=-1)
    beta = _to_chunks(beta.astype(f32), c)[..., None]
    causal = jnp.tril(jnp.ones((c, c), bool))
    eye = jnp.eye(c, dtype=f32)
    diff = g[..., :, None] - g[..., None, :]
    gamma = jnp.where(causal, jnp.exp(jnp.where(causal, diff, 0.0)), 0.0)
    kb = k * beta
    m = jnp.einsum('bnhid,bnhjd->bnhij', kb, k) * gamma * (1.0 - eye)
    rhs = jnp.concatenate([v * beta, kb * jnp.exp(g)[..., None]], -1)
    sol = lax.linalg.triangular_solve(m + eye, rhs, left_side=True, lower=True, unit_diagonal=True)
    u, w = sol[..., :dv], sol[..., dv:]
    qk = jnp.einsum('bnhid,bnhjd->bnhij', q, k) * gamma
    q_dec = q * jnp.exp(g)[..., None]
    k_dec = k * jnp.exp(g[..., -1:] - g)[..., None]
    chunk_decay = jnp.exp(g[..., -1])[..., None, None]

    def step(state, xs):
        qk_n, qd_n, kd_n, u_n, w_n, dec_n = xs
        v_new = u_n - jnp.einsum('bhcd,bhde->bhce', w_n, state)
        o = jnp.einsum('bhcd,bhde->bhce', qd_n, state) + jnp.einsum('bhij,bhje->bhie', qk_n, v_new)
        state = state * dec_n + jnp.einsum('bhcd,bhce->bhde', kd_n, v_new)
        return state, o

    xs = tuple(jnp.moveaxis(t, 1, 0) for t in (qk, q_dec, k_dec, u, w, chunk_decay))
    state0 = jnp.zeros((q.shape[0], q.shape[2], dk, dv), f32)
    _, o = lax.scan(step, state0, xs)
    return _from_chunks(jnp.moveaxis(o, 0, 1))


def _forgetting_attention(q, k, v, log_f):
    f32 = jnp.float32
    b, s, h, d = q.shape
    nb = s // FOX_BLOCK
    cum = jnp.cumsum(log_f.astype(f32), axis=1).transpose(0, 2, 1)
    qb = jnp.moveaxis(q.reshape(b, nb, FOX_BLOCK, h, d), 1, 0)
    cb = jnp.moveaxis(cum.reshape(b, h, nb, FOX_BLOCK), 2, 0)
    key_pos = jnp.arange(s)
    scale = d ** -0.5

    def block(xs):
        q_i, c_i, i = xs
        logits = jnp.einsum('bqhd,bkhd->bhqk', q_i, k, preferred_element_type=f32) * scale
        q_pos = i * FOX_BLOCK + jnp.arange(FOX_BLOCK)
        allowed = key_pos[None, :] <= q_pos[:, None]
        bias = jnp.where(allowed, c_i[..., :, None] - cum[:, :, None, :], 0.0)
        logits = jnp.where(allowed, logits + bias, MASK_VALUE)
        p = jax.nn.softmax(logits, axis=-1)
        return jnp.einsum('bhqk,bkhd->bqhd', p.astype(v.dtype), v)

    o = lax.map(block, (qb, cb, jnp.arange(nb)))
    return jnp.moveaxis(o, 0, 1).reshape(b, s, h, d)


def _hgrn2_recurrence(q, k, v, log_f):
    f32 = jnp.float32
    c = HGRN_CHUNK
    q, k, v, lf = (_to_chunks(t.astype(f32), c) for t in (q, k, v, log_f))
    bcum = jnp.cumsum(lf, axis=-2)
    ref = bcum[..., c // 2 - 1:c // 2, :]
    causal = jnp.tril(jnp.ones((c, c), bool))
    a = jnp.einsum('bnhid,bnhjd->bnhij', q * jnp.exp(bcum - ref), k * jnp.exp(ref - bcum))
    a = jnp.where(causal, a, 0.0)
    o_intra = jnp.einsum('bnhij,bnhje->bnhie', a, v)
    q_in = q * jnp.exp(bcum)
    k_st = k * jnp.exp(bcum[..., -1:, :] - bcum)
    dec = jnp.exp(bcum[..., -1, :])[..., None]

    def step(state, xs):
        q_n, k_n, v_n, d_n = xs
        o = jnp.einsum('bhcd,bhde->bhce', q_n, state)
        state = state * d_n + jnp.einsum('bhcd,bhce->bhde', k_n, v_n)
        return state, o

    xs = tuple(jnp.moveaxis(t, 1, 0) for t in (q_in, k_st, v, dec))
    state0 = jnp.zeros((q.shape[0], q.shape[2], q.shape[-1], v.shape[-1]), f32)
    _, o_inter = lax.scan(step, state0, xs)
    return _from_chunks(o_intra + jnp.moveaxis(o_inter, 0, 1))


def _top2_moe(x, router_w, wg, wu, wd):
    f32 = jnp.float32
    b, s, d = x.shape
    xt = x.reshape(b * s, d)
    logits = jnp.dot(xt, router_w, preferred_element_type=f32)
    top_v, top_i = lax.top_k(logits, TOP_K)
    gates = jax.nn.softmax(top_v, axis=-1)
    combine = jnp.sum(jax.nn.one_hot(top_i, N_EXPERTS, dtype=f32) * gates[..., None], axis=1)
    y = jnp.zeros_like(xt)
    for e in range(N_EXPERTS):
        h = jax.nn.silu(xt @ wg[e]) * (xt @ wu[e])
        y = y + combine[:, e:e + 1].astype(xt.dtype) * (h @ wd[e])
    return y.reshape(b, s, d)


def setup_inputs(seed: int = 0) -> dict:
    key = jax.random.key(seed)
    ks = jax.random.split(key, 24)
    f32 = jnp.float32
    d = D_MODEL
    beta = DEEPNORM_BETA

    def nrm(k, shape, scale):
        return jax.random.normal(k, shape, f32) * scale

    starts = (0,) + IN_OFFSETS
    col_scale = np.ones((IN_COLS,), np.float32)
    for slot in VALUE_SLOTS:
        col_scale[starts[slot]:starts[slot] + IN_SPLITS[slot]] = beta
    dt = jnp.exp(jax.random.uniform(ks[4], (DEPTH, GDN_HEADS), f32, np.log(1e-3), np.log(1e-1)))
    return {
        'x': nrm(ks[0], (BATCH, SEQ, d), 1.0),
        'mem': nrm(ks[1], (BATCH, MEM_LEN, d), 1.0),
        'w_in': nrm(ks[2], (DEPTH, d, IN_COLS), d ** -0.5) * jnp.asarray(col_scale),
        'conv_w': nrm(ks[3], (DEPTH, CONV_K, 3 * GDN_W), CONV_K ** -0.5),
        'gdn_a_log': jnp.log(jax.random.uniform(ks[5], (DEPTH, GDN_HEADS), f32, 1.0, 16.0)),
        'gdn_dt_bias': dt + jnp.log(-jnp.expm1(-dt)),
        'gdn_norm_g': 1.0 + nrm(ks[6], (DEPTH, HEAD_DIM), 0.02),
        'fox_f_bias': jax.random.uniform(ks[7], (DEPTH, FOX_HEADS), f32, 1.0, 4.0),
        'hgrn_lb_logits': nrm(ks[8], (DEPTH, HGRN_W), 0.5),
        'hgrn_norm_g': 1.0 + nrm(ks[9], (DEPTH, HEAD_DIM), 0.02),
        'w_out': nrm(ks[10], (DEPTH, d, d), beta * d ** -0.5),
        'xa_wq': nrm(ks[11], (DEPTH, d, d), d ** -0.5),
        'xa_wk': nrm(ks[12], (DEPTH, d, d), d ** -0.5),
        'xa_wv': nrm(ks[13], (DEPTH, d, d), beta * d ** -0.5),
        'xa_wo': nrm(ks[14], (DEPTH, d, d), beta * d ** -0.5),
        'ln_g': 1.0 + nrm(ks[15], (DEPTH, 3, d), 0.02),
        'ln_b': nrm(ks[16], (DEPTH, 3, d), 0.02),
        'ffn_wg': nrm(ks[17], (N_DENSE, d, D_FF), beta * d ** -0.5),
        'ffn_wu': nrm(ks[18], (N_DENSE, d, D_FF), beta * d ** -0.5),
        'ffn_wd': nrm(ks[19], (N_DENSE, D_FF, d), beta * D_FF ** -0.5),
        'router_w': nrm(ks[20], (N_MOE, d, N_EXPERTS), d ** -0.5),
        'moe_wg': nrm(ks[21], (N_MOE, N_EXPERTS, d, EXPERT_FF), beta * d ** -0.5),
        'moe_wu': nrm(ks[22], (N_MOE, N_EXPERTS, d, EXPERT_FF), beta * d ** -0.5),
        'moe_wd': nrm(ks[23], (N_MOE, N_EXPERTS, EXPERT_FF, d), beta * EXPERT_FF ** -0.5),
    }


def reference(x, mem, w_in, conv_w, gdn_a_log, gdn_dt_bias, gdn_norm_g, fox_f_bias,
              hgrn_lb_logits, hgrn_norm_g, w_out, xa_wq, xa_wk, xa_wv, xa_wo, ln_g, ln_b,
              ffn_wg, ffn_wu, ffn_wd, router_w, moe_wg, moe_wu, moe_wd):
    f32 = jnp.float32
    b, s, d = x.shape
    m_len = mem.shape[1]
    alpha = DEEPNORM_ALPHA
    lb_w = jax.nn.softmax(hgrn_lb_logits.astype(f32), axis=0)
    lower_bounds = jnp.cumsum(lb_w, axis=0) - lb_w[:1]
    for l in range(DEPTH):
        (qa, ka, va, za, beta_a, dec_a, qb, kb, vb, fb,
         qc, fc, ic, gc) = jnp.split(x @ w_in[l], IN_OFFSETS, axis=-1)
        qkv = jax.nn.silu(_causal_depthwise_conv(jnp.concatenate([qa, ka, va], -1), conv_w[l]))
        qa, ka, va = jnp.split(qkv, (GDN_W, 2 * GDN_W), axis=-1)
        log_a = -jnp.exp(gdn_a_log[l].astype(f32)) * jax.nn.softplus(
            dec_a.astype(f32) + gdn_dt_bias[l].astype(f32))
        o_a = _gated_delta_rule(_l2_normalize(_heads(qa, GDN_HEADS)), _l2_normalize(_heads(ka, GDN_HEADS)),
                                _heads(va, GDN_HEADS), log_a, jax.nn.sigmoid(beta_a.astype(f32)))
        o_a = _rms_norm(o_a, gdn_norm_g[l]) * jax.nn.silu(_heads(za, GDN_HEADS).astype(f32))
        log_f = jax.nn.log_sigmoid(fb.astype(f32) + fox_f_bias[l].astype(f32))
        o_b = _forgetting_attention(_heads(qb, FOX_HEADS), _heads(kb, FOX_HEADS), _heads(vb, FOX_HEADS), log_f)
        lb = lower_bounds[l].reshape(HGRN_HEADS, HEAD_DIM)
        fz = _heads(fc, HGRN_HEADS).astype(f32)
        f_c = lb + (1.0 - lb) * jax.nn.sigmoid(fz)
        log_fc = jnp.log(f_c)
        k_c = 1.0 - f_c
        o_c = _hgrn2_recurrence(jax.nn.silu(_heads(qc, HGRN_HEADS).astype(f32)), k_c,
                                _heads(ic, HGRN_HEADS), log_fc)
        o_c = _rms_norm(o_c, hgrn_norm_g[l]) * jax.nn.sigmoid(_heads(gc, HGRN_HEADS).astype(f32))
        mix = jnp.concatenate([o_a.reshape(b, s, GDN_W), o_b.reshape(b, s, FOX_W).astype(f32),
                               o_c.reshape(b, s, HGRN_W)], axis=-1).astype(x.dtype)
        x = _layer_norm(alpha * x + mix @ w_out[l], ln_g[l, 0], ln_b[l, 0])
        qx = (x @ xa_wq[l]).reshape(b, s, XA_HEADS, XA_HEAD_DIM)
        km = (mem @ xa_wk[l]).reshape(b, m_len, XA_HEADS, XA_HEAD_DIM)
        vm = (mem @ xa_wv[l]).reshape(b, m_len, XA_HEADS, XA_HEAD_DIM)
        logits = jnp.einsum('bshd,bmhd->bhsm', qx, km, preferred_element_type=f32) * XA_HEAD_DIM ** -0.5
        p = jax.nn.softmax(logits, axis=-1).astype(x.dtype)
        xo = jnp.einsum('bhsm,bmhd->bshd', p, vm).reshape(b, s, d)
        x = _layer_norm(alpha * x + xo @ xa_wo[l], ln_g[l, 1], ln_b[l, 1])
        if l % 2 == 0:
            i = l // 2
            y = (jax.nn.silu(x @ ffn_wg[i]) * (x @ ffn_wu[i])) @ ffn_wd[i]
        else:
            i = l // 2
            y = _top2_moe(x, router_w[i], moe_wg[i], moe_wu[i], moe_wd[i])
        x = _layer_norm(alpha * x + y, ln_g[l, 2], ln_b[l, 2])
    return x
```

```python
import functools

import jax
import jax.numpy as jnp
from jax import lax
from jax.experimental import pallas as pl
from jax.experimental.pallas import tpu as pltpu

F32 = jnp.float32
BF16 = jnp.bfloat16
HI = lax.Precision.HIGHEST

HEAD_DIM = 128
CONV_K = 4
XA_HEADS = 4
N_EXPERTS = 8
TOP_K = 2
LN_EPS = 1e-5
RMS_EPS = 1e-6
L2_EPS = 1e-6
MASK_VALUE = -1e30
LANES = 128
SUBLANES = 8
VMEM_LIMIT = 56 * 1024 * 1024

GATE_BETA, GATE_DEC, GATE_FOX = 0, 8, 16


def _cparams(*sem):
    return pltpu.CompilerParams(dimension_semantics=sem, vmem_limit_bytes=VMEM_LIMIT)


def _pick(n, pref):
    t = min(pref, n)
    while n % t:
        t //= 2
    return t


def _dot(a, b):
    return jnp.dot(a, b, preferred_element_type=F32)


def _dot_nt(a, b):
    return lax.dot_general(a, b, (((1,), (1,)), ((), ())), preferred_element_type=F32)


def _dot_hi(a, b):
    return jnp.dot(a, b, preferred_element_type=F32, precision=HI)


def _sigmoid(x):
    return 1.0 / (1.0 + jnp.exp(-x))


def _silu(x):
    return x * _sigmoid(x)


def _softplus(x):
    return jnp.maximum(x, 0.0) + jnp.log(1.0 + jnp.exp(-jnp.abs(x)))


def _layer_norm(y, g, b):
    mu = jnp.mean(y, axis=-1, keepdims=True)
    d = y - mu
    var = jnp.mean(d * d, axis=-1, keepdims=True)
    return d * lax.rsqrt(var + LN_EPS) * g + b


def _mm_kernel(x_ref, w_ref, o_ref, *, hi):
    if hi:
        acc = _dot_hi(x_ref[...], w_ref[...])
    else:
        acc = _dot(x_ref[...], w_ref[...])
    o_ref[...] = acc.astype(o_ref.dtype)


def _matmul(x, w, out_dtype, *, tm=1024, tn=512, hi=False):
    m, k = x.shape
    n = w.shape[1]
    tm, tn = _pick(m, tm), _pick(n, tn)
    return pl.pallas_call(
        functools.partial(_mm_kernel, hi=hi),
        grid=(m // tm, n // tn),
        in_specs=[pl.BlockSpec((tm, k), lambda i, j: (i, 0)),
                  pl.BlockSpec((k, tn), lambda i, j: (0, j))],
        out_specs=pl.BlockSpec((tm, tn), lambda i, j: (i, j)),
        out_shape=jax.ShapeDtypeStruct((m, n), out_dtype),
        compiler_params=_cparams("parallel", "arbitrary"),
        name="matmul_hi" if hi else "matmul",
    )(x, w)


def _mm_res_ln_kernel(*refs, alpha, n_in):
    a_refs, w_refs = refs[:n_in], refs[n_in:2 * n_in]
    r_ref, g_ref, b_ref, of_ref, ob_ref = refs[2 * n_in:]
    y = alpha * r_ref[...]
    for a_ref, w_ref in zip(a_refs, w_refs):
        y = y + _dot(a_ref[...], w_ref[...])
    out = _layer_norm(y, g_ref[...], b_ref[...])
    of_ref[...] = out
    ob_ref[...] = out.astype(BF16)


def _matmul_res_ln(a_list, w, w_row_blocks, resid, g, b, alpha, *, tm=256):
    m, d = resid.shape
    tm = _pick(m, tm)
    n_in = len(a_list)
    row = lambda i: (i, 0)
    fixed = lambda i: (0, 0)
    a_specs = [pl.BlockSpec((tm, a.shape[1]), row) for a in a_list]
    w_specs = [pl.BlockSpec((rows, d), functools.partial(lambda i, bi: (bi, 0), bi=bi))
               for rows, bi in w_row_blocks]
    return pl.pallas_call(
        functools.partial(_mm_res_ln_kernel, alpha=alpha, n_in=n_in),
        grid=(m // tm,),
        in_specs=a_specs + w_specs + [pl.BlockSpec((tm, d), row), pl.BlockSpec((1, d), fixed),
                                      pl.BlockSpec((1, d), fixed)],
        out_specs=[pl.BlockSpec((tm, d), row), pl.BlockSpec((tm, d), row)],
        out_shape=[jax.ShapeDtypeStruct((m, d), F32), jax.ShapeDtypeStruct((m, d), BF16)],
        compiler_params=_cparams("parallel"),
        name="matmul_res_ln",
    )(*a_list, *([w] * n_in), resid, g.reshape(1, d), b.reshape(1, d))


def _ffn_kernel(x_ref, wg_ref, wu_ref, wd_ref, c_ref, r_ref, g_ref, b_ref,
                of_ref, ob_ref, acc_ref, *, alpha, gated):
    e, j = pl.program_id(1), pl.program_id(2)

    @pl.when((e == 0) & (j == 0))
    def _():
        acc_ref[...] = jnp.zeros_like(acc_ref)

    x = x_ref[...]
    h = _silu(_dot(x, wg_ref[...])) * _dot(x, wu_ref[...])
    y = _dot(h.astype(BF16), wd_ref[...])
    if gated:
        lane = lax.broadcasted_iota(jnp.int32, c_ref.shape, 1)
        ce = jnp.sum(jnp.where(lane == e, c_ref[...], 0.0), axis=-1, keepdims=True)
        y = ce * y
    acc_ref[...] += y

    @pl.when((e == pl.num_programs(1) - 1) & (j == pl.num_programs(2) - 1))
    def _():
        out = _layer_norm(alpha * r_ref[...] + acc_ref[...], g_ref[...], b_ref[...])
        of_ref[...] = out
        ob_ref[...] = out.astype(BF16)


def _ffn(xb, wg, wu, wd, combine, resid, g, b, alpha, *, tm=512, tf=512):
    m, d = xb.shape
    ne, _, f = wg.shape
    tm, tf = _pick(m, tm), _pick(f, tf)
    gated = combine is not None
    if not gated:
        combine = jnp.ones((m, LANES), F32)
    row = lambda i, e, j: (i, 0)
    fixed = lambda i, e, j: (0, 0)
    return pl.pallas_call(
        functools.partial(_ffn_kernel, alpha=alpha, gated=gated),
        grid=(m // tm, ne, f // tf),
        in_specs=[pl.BlockSpec((tm, d), row),
                  pl.BlockSpec((None, d, tf), lambda i, e, j: (e, 0, j)),
                  pl.BlockSpec((None, d, tf), lambda i, e, j: (e, 0, j)),
                  pl.BlockSpec((None, tf, d), lambda i, e, j: (e, j, 0)),
                  pl.BlockSpec((tm, LANES), row),
                  pl.BlockSpec((tm, d), row),
                  pl.BlockSpec((1, d), fixed), pl.BlockSpec((1, d), fixed)],
        out_specs=[pl.BlockSpec((tm, d), row), pl.BlockSpec((tm, d), row)],
        out_shape=[jax.ShapeDtypeStruct((m, d), F32), jax.ShapeDtypeStruct((m, d), BF16)],
        scratch_shapes=[pltpu.VMEM((tm, d), F32)],
        compiler_params=_cparams("parallel", "arbitrary", "arbitrary"),
        name="ffn_gated" if gated else "ffn_dense",
    )(xb, wg, wu, wd, combine, resid, g.reshape(1, d), b.reshape(1, d))


def _route_kernel(l_ref, c_ref):
    logits = l_ref[...]
    lane = lax.broadcasted_iota(jnp.int32, logits.shape, 1)
    logits = jnp.where(lane < N_EXPERTS, logits, -jnp.inf)
    m1 = jnp.max(logits, axis=-1, keepdims=True)
    i1 = jnp.min(jnp.where(logits == m1, lane, LANES), axis=-1, keepdims=True)
    rest = jnp.where(lane == i1, -jnp.inf, logits)
    m2 = jnp.max(rest, axis=-1, keepdims=True)
    i2 = jnp.min(jnp.where(rest == m2, lane, LANES), axis=-1, keepdims=True)
    e2 = jnp.exp(m2 - m1)
    g1 = 1.0 / (1.0 + e2)
    g2 = e2 / (1.0 + e2)
    c_ref[...] = jnp.where(lane == i1, g1, 0.0) + jnp.where(lane == i2, g2, 0.0)


def _route(logits, *, tm=1024):
    m = logits.shape[0]
    tm = _pick(m, tm)
    spec = pl.BlockSpec((tm, LANES), lambda i: (i, 0))
    return pl.pallas_call(
        _route_kernel, grid=(m // tm,), in_specs=[spec], out_specs=spec,
        out_shape=jax.ShapeDtypeStruct((m, LANES), F32),
        compiler_params=_cparams("parallel"), name="route_top2",
    )(logits)


def _xattn_kernel(x_ref, wq_ref, k_ref, v_ref, o_ref):
    d = wq_ref.shape[1]
    hd = d // XA_HEADS
    q = _dot(x_ref[...], wq_ref[...]).astype(BF16)
    for h in range(XA_HEADS):
        sl = slice(h * hd, (h + 1) * hd)
        s = _dot_nt(q[:, sl], k_ref[:, sl]) * (hd ** -0.5)
        s = s - jnp.max(s, axis=-1, keepdims=True)
        p = jnp.exp(s)
        p = p / jnp.sum(p, axis=-1, keepdims=True)
        o_ref[:, sl] = _dot(p.astype(BF16), v_ref[:, sl]).astype(o_ref.dtype)


def _xattn(xb, wq, km, vm, nb, *, tm=256):
    t, d = xb.shape
    s = t // nb
    ml = km.shape[1]
    tm = _pick(s, tm)
    ns = s // tm
    return pl.pallas_call(
        _xattn_kernel,
        grid=(nb, ns),
        in_specs=[pl.BlockSpec((tm, d), lambda b, i: (b * ns + i, 0)),
                  pl.BlockSpec((d, d), lambda b, i: (0, 0)),
                  pl.BlockSpec((None, ml, d), lambda b, i: (b, 0, 0)),
                  pl.BlockSpec((None, ml, d), lambda b, i: (b, 0, 0))],
        out_specs=pl.BlockSpec((tm, d), lambda b, i: (b * ns + i, 0)),
        out_shape=jax.ShapeDtypeStruct((t, d), BF16),
        compiler_params=_cparams("parallel", "arbitrary"),
        name="xattn",
    )(xb, wq, km, vm)


def _fox_cum_kernel(gt_ref, bias_ref, c_ref, carry_ref):
    @pl.when(pl.program_id(1) == 0)
    def _():
        carry_ref[...] = jnp.zeros_like(carry_ref)

    z = gt_ref[...] + bias_ref[...]
    lf = -_softplus(-z)
    tk = z.shape[1]
    r = lax.broadcasted_iota(jnp.int32, (tk, tk), 0)
    c = lax.broadcasted_iota(jnp.int32, (tk, tk), 1)
    upper = (r <= c).astype(F32)
    cum = _dot_hi(lf, upper) + carry_ref[...]
    c_ref[...] = cum
    carry_ref[...] = cum[:, tk - 1:tk]


def _fox_cum(gates_t, bias, nb, *, tk=512):
    t = gates_t.shape[1]
    s = t // nb
    tk = _pick(s, tk)
    ns = s // tk
    rb = GATE_FOX // SUBLANES
    return pl.pallas_call(
        _fox_cum_kernel,
        grid=(nb, ns),
        in_specs=[pl.BlockSpec((SUBLANES, tk), lambda b, i: (rb, b * ns + i)),
                  pl.BlockSpec((SUBLANES, 1), lambda b, i: (0, 0))],
        out_specs=pl.BlockSpec((None, SUBLANES, tk), lambda b, i: (b, 0, i)),
        out_shape=jax.ShapeDtypeStruct((nb, SUBLANES, s), F32),
        scratch_shapes=[pltpu.VMEM((SUBLANES, 1), F32)],
        compiler_params=_cparams("parallel", "arbitrary"),
        name="fox_cumsum",
    )(gates_t, bias)


def _fox_kernel(q_ref, k_ref, v_ref, c_ref, o_ref, *, tq, tk, scale):
    qi = pl.program_id(2)
    q = q_ref[...]
    q0 = pl.multiple_of(qi * tq, tq)
    c_q0 = c_ref[:, pl.ds(q0, LANES)][:, 0:1]
    row = lax.broadcasted_iota(jnp.int32, (tq, tk), 0) + q0
    col = lax.broadcasted_iota(jnp.int32, (tq, tk), 1)

    def body(kj, carry):
        m, l, acc = carry
        k0 = pl.multiple_of(kj * tk, tk)
        kb = k_ref[pl.ds(k0, tk), :]
        vb = v_ref[pl.ds(k0, tk), :]
        s = _dot_nt(q, kb) * scale + (c_q0 - c_ref[:, pl.ds(k0, tk)])
        s = jnp.where(col + k0 <= row, s, MASK_VALUE)
        m_new = jnp.maximum(m, jnp.max(s, axis=-1, keepdims=True))
        a = jnp.exp(m - m_new)
        p = jnp.exp(s - m_new)
        l = a * l + jnp.sum(p, axis=-1, keepdims=True)
        acc = a * acc + _dot(p.astype(BF16), vb)
        return m_new, l, acc

    init = (jnp.full((tq, 1), MASK_VALUE, F32), jnp.zeros((tq, 1), F32),
            jnp.zeros((tq, HEAD_DIM), F32))
    nk = (q0 + tq + tk - 1) // tk
    m, l, acc = lax.fori_loop(0, nk, body, init)
    o_ref[...] = (acc / l).astype(o_ref.dtype)


def _fox_attention(proj, cum, nb, nh, *, tq=512, tk=512):
    t = proj.shape[0]
    s = t // nb
    tq, tk = _pick(s, tq), _pick(s, tk)
    nq = s // tq
    proj3 = proj.reshape(nb, s, proj.shape[1])
    cum2 = cum.reshape(nb * SUBLANES, 1, s)
    out = pl.pallas_call(
        functools.partial(_fox_kernel, tq=tq, tk=tk, scale=HEAD_DIM ** -0.5),
        grid=(nb, nh, nq),
        in_specs=[pl.BlockSpec((None, tq, HEAD_DIM), lambda b, h, i: (b, i, h)),
                  pl.BlockSpec((None, s, HEAD_DIM), lambda b, h, i: (b, 0, nh + h)),
                  pl.BlockSpec((None, s, HEAD_DIM), lambda b, h, i: (b, 0, 2 * nh + h)),
                  pl.BlockSpec((None, 1, s), lambda b, h, i: (b * SUBLANES + h, 0, 0))],
        out_specs=pl.BlockSpec((None, tq, HEAD_DIM), lambda b, h, i: (b, i, h)),
        out_shape=jax.ShapeDtypeStruct((nb, s, nh * HEAD_DIM), BF16),
        compiler_params=_cparams("parallel", "parallel", "arbitrary"),
        name="fox_attention",
    )(proj3, proj3, proj3, cum2)
    return out.reshape(t, nh * HEAD_DIM)


HGRN_SUB = 16


def _hgrn_kernel(p_ref, lb_ref, ng_ref, tri_ref, o_ref, st_ref, *, nh, c):
    @pl.when(pl.program_id(1) == 0)
    def _():
        st_ref[...] = jnp.zeros_like(st_ref)

    w = nh * HEAD_DIM
    row = lax.broadcasted_iota(jnp.int32, (c, c), 0)
    col = lax.broadcasted_iota(jnp.int32, (c, c), 1)
    tri = tri_ref[...]
    for h in range(nh):
        sl = lambda grp: slice(grp * w + h * HEAD_DIM, grp * w + (h + 1) * HEAD_DIM)
        q = _silu(p_ref[:, sl(0)].astype(F32))
        lb = lb_ref[:, h * HEAD_DIM:(h + 1) * HEAD_DIM]
        f = lb + (1.0 - lb) * _sigmoid(p_ref[:, sl(1)].astype(F32))
        k = 1.0 - f
        vb = p_ref[:, sl(2)]
        gate = _sigmoid(p_ref[:, sl(3)].astype(F32))
        bc = _dot_hi(tri, jnp.log(f))

        nsub = c // HGRN_SUB
        b3 = bc.reshape(nsub, HGRN_SUB, HEAD_DIM)
        ref = b3[:, HGRN_SUB // 2 - 1:HGRN_SUB // 2, :]
        qd = (q.reshape(b3.shape) * jnp.exp(b3 - ref)).reshape(c, HEAD_DIM)
        kd = (k.reshape(b3.shape) * jnp.exp(ref - b3)).reshape(c, HEAD_DIM)
        same = (row // HGRN_SUB == col // HGRN_SUB) & (col <= row)
        a = jnp.where(same, _dot_nt(qd.astype(BF16), kd.astype(BF16)), 0.0)
        s = HGRN_SUB
        while s < c:
            n = c // s
            b3 = bc.reshape(n, s, HEAD_DIM)
            last = b3[:, s - 1:s, :]
            prev = jnp.concatenate([jnp.zeros_like(last[:1]), last[:-1]], axis=0)
            qs = (q.reshape(b3.shape) * jnp.exp(b3 - prev)).reshape(c, HEAD_DIM)
            ks = (k.reshape(b3.shape) * jnp.exp(last - b3)).reshape(c, HEAD_DIM)
            rb, cb = row // s, col // s
            pair = (rb % 2 == 1) & (cb == rb - 1)
            a = a + jnp.where(pair, _dot_nt(qs.astype(BF16), ks.astype(BF16)), 0.0)
            s *= 2

        st = st_ref[h]
        o = _dot(a.astype(BF16), vb)
        o = o + _dot_nt((q * jnp.exp(bc)).astype(BF16), st.astype(BF16))
        b_last = bc[c - 1:c, :]
        k_st = k * jnp.exp(b_last - bc)
        st_ref[h] = st * jnp.exp(b_last) + _dot(vb.astype(F32).T.astype(BF16), k_st.astype(BF16))

        o = o * lax.rsqrt(jnp.mean(o * o, axis=-1, keepdims=True) + RMS_EPS) * ng_ref[...]
        o_ref[:, h * HEAD_DIM:(h + 1) * HEAD_DIM] = (o * gate).astype(o_ref.dtype)


def _tri_lower(c):
    r = lax.broadcasted_iota(jnp.int32, (c, c), 0)
    col = lax.broadcasted_iota(jnp.int32, (c, c), 1)
    return (col <= r).astype(F32)


def _hgrn(proj, lb, norm_g, nb, nh, *, c=128):
    t, wtot = proj.shape
    s = t // nb
    c = _pick(s, c)
    ns = s // c
    w = nh * HEAD_DIM
    fixed = lambda b, i: (0, 0)
    return pl.pallas_call(
        functools.partial(_hgrn_kernel, nh=nh, c=c),
        grid=(nb, ns),
        in_specs=[pl.BlockSpec((c, wtot), lambda b, i: (b * ns + i, 0)),
                  pl.BlockSpec((1, w), fixed), pl.BlockSpec((1, HEAD_DIM), fixed),
                  pl.BlockSpec((c, c), fixed)],
        out_specs=pl.BlockSpec((c, w), lambda b, i: (b * ns + i, 0)),
        out_shape=jax.ShapeDtypeStruct((t, w), BF16),
        scratch_shapes=[pltpu.VMEM((nh, HEAD_DIM, HEAD_DIM), F32)],
        compiler_params=_cparams("parallel", "arbitrary"),
        name="hgrn2",
    )(proj, lb, norm_g.reshape(1, HEAD_DIM), _tri_lower(c))


def _level_masks(c):
    r = lax.broadcasted_iota(jnp.int32, (c, c), 0)
    col = lax.broadcasted_iota(jnp.int32, (c, c), 1)
    out, s = [], 1
    while s < c:
        rb, cb = r // s, col // s
        out.append(((rb % 2 == 1) & (cb == rb - 1)).astype(F32))
        s *= 2
    return jnp.stack(out)


def _gdn_kernel(p_ref, halo_ref, gt_ref, gtt_ref, cw_ref, av_ref, dtv_ref, avt_ref, dtt_ref,
                ng_ref, tri_ref, lvl_ref, o_ref, s_ref, *, nh, c):
    first = pl.program_id(1) == 0

    @pl.when(first)
    def _():
        s_ref[...] = jnp.zeros_like(s_ref)

    w = nh * HEAD_DIM
    row = lax.broadcasted_iota(jnp.int32, (c, c), 0)
    col = lax.broadcasted_iota(jnp.int32, (c, c), 1)
    causal = col <= row
    tri = tri_ref[...]
    gates = gt_ref[...]
    la = -av_ref[...] * _softplus(gates + dtv_ref[...])
    g_col = _dot_hi(tri, la)
    la_t = -avt_ref[...] * _softplus(gtt_ref[...] + dtt_ref[...])
    g_row = lax.dot_general(la_t, tri, (((1,), (1,)), ((), ())),
                            preferred_element_type=F32, precision=HI)
    halo_scale = jnp.where(first, 0.0, 1.0)
    sub = lax.broadcasted_iota(jnp.int32, (SUBLANES, HEAD_DIM), 0)

    def conv_silu(grp, h):
        lo = grp * w + h * HEAD_DIM
        cur = p_ref[:, lo:lo + HEAD_DIM].astype(F32)
        halo = halo_ref[:, lo:lo + HEAD_DIM].astype(F32) * halo_scale
        cw = cw_ref[:, lo:lo + HEAD_DIM]
        y = cw[CONV_K - 1:CONV_K, :] * cur
        for j in range(1, CONV_K):
            rolled = pltpu.roll(cur, j, axis=0)
            head = jnp.where(sub < j, pltpu.roll(halo, j, axis=0), rolled[:SUBLANES])
            shifted = jnp.concatenate([head, rolled[SUBLANES:]], axis=0)
            y = y + cw[CONV_K - 1 - j:CONV_K - j, :] * shifted
        return _silu(y)

    def l2n(x):
        return x * lax.rsqrt(jnp.sum(x * x, axis=-1, keepdims=True) + L2_EPS)

    for h in range(nh):
        q = l2n(conv_silu(0, h)) * (HEAD_DIM ** -0.5)
        k = l2n(conv_silu(1, h))
        v = conv_silu(2, h)
        beta = _sigmoid(gates[:, GATE_BETA + h:GATE_BETA + h + 1])
        gc = g_col[:, GATE_DEC + h:GATE_DEC + h + 1]
        gr = g_row[h:h + 1, :]
        gamma = jnp.where(causal, jnp.exp(jnp.where(causal, gc - gr, 0.0)), 0.0)
        kb = k * beta
        kbf = k.astype(BF16)
        m = _dot_nt(kb.astype(BF16), kbf) * gamma

        n = -(m * lvl_ref[0])
        for lv in range(1, lvl_ref.shape[0]):
            low = m * lvl_ref[lv]
            p = low + _dot(n.astype(BF16), low.astype(BF16))
            n = n - p - _dot(p.astype(BF16), n.astype(BF16))

        eg = jnp.exp(gc)
        g_last = gc[c - 1:c, :]
        rhs = jnp.concatenate([v * beta, kb * eg], axis=1)
        sol = rhs + _dot(n.astype(BF16), rhs.astype(BF16))
        u, wmat = sol[:, :HEAD_DIM], sol[:, HEAD_DIM:]

        state = s_ref[h]
        sb = state.astype(BF16)
        v_new = u - _dot(wmat.astype(BF16), sb)
        qk = _dot_nt(q.astype(BF16), kbf) * gamma
        o = _dot((q * eg).astype(BF16), sb) + _dot(qk.astype(BF16), v_new.astype(BF16))
        k_dec = k * jnp.exp(g_last - gc)
        s_ref[h] = state * jnp.exp(g_last) + _dot(k_dec.T.astype(BF16), v_new.astype(BF16))

        z = p_ref[:, 3 * w + h * HEAD_DIM:3 * w + (h + 1) * HEAD_DIM].astype(F32)
        o = o * lax.rsqrt(jnp.mean(o * o, axis=-1, keepdims=True) + RMS_EPS) * ng_ref[...]
        o_ref[:, h * HEAD_DIM:(h + 1) * HEAD_DIM] = (o * _silu(z)).astype(o_ref.dtype)


def _gdn(proj, gates, gates_t, conv_w, a_log, dt_bias, norm_g, nb, nh, *, c=256):
    t, wtot = proj.shape
    s = t // nb
    c = _pick(s, c)
    ns = s // c
    w = nh * HEAD_DIM
    a = jnp.exp(a_log.astype(F32))
    av = jnp.zeros((1, LANES), F32).at[0, GATE_DEC:GATE_DEC + nh].set(a)
    dtv = jnp.zeros((1, LANES), F32).at[0, GATE_DEC:GATE_DEC + nh].set(dt_bias.astype(F32))
    avt = jnp.zeros((SUBLANES, 1), F32).at[:nh, 0].set(a)
    dtt = jnp.zeros((SUBLANES, 1), F32).at[:nh, 0].set(dt_bias.astype(F32))
    lvl = _level_masks(c)
    fixed = lambda b, i: (0, 0)
    rows_per_halo = c // SUBLANES
    return pl.pallas_call(
        functools.partial(_gdn_kernel, nh=nh, c=c),
        grid=(nb, ns),
        in_specs=[pl.BlockSpec((c, wtot), lambda b, i: (b * ns + i, 0)),
                  pl.BlockSpec((SUBLANES, 3 * w),
                               lambda b, i: (jnp.maximum((b * ns + i) * rows_per_halo - 1, 0), 0)),
                  pl.BlockSpec((c, LANES), lambda b, i: (b * ns + i, 0)),
                  pl.BlockSpec((SUBLANES, c), lambda b, i: (GATE_DEC // SUBLANES, b * ns + i)),
                  pl.BlockSpec((CONV_K, 3 * w), fixed),
                  pl.BlockSpec((1, LANES), fixed), pl.BlockSpec((1, LANES), fixed),
                  pl.BlockSpec((SUBLANES, 1), fixed), pl.BlockSpec((SUBLANES, 1), fixed),
                  pl.BlockSpec((1, HEAD_DIM), fixed),
                  pl.BlockSpec((c, c), fixed),
                  pl.BlockSpec(lvl.shape, lambda b, i: (0, 0, 0))],
        out_specs=pl.BlockSpec((c, w), lambda b, i: (b * ns + i, 0)),
        out_shape=jax.ShapeDtypeStruct((t, w), BF16),
        scratch_shapes=[pltpu.VMEM((nh, HEAD_DIM, HEAD_DIM), F32)],
        compiler_params=_cparams("parallel", "arbitrary"),
        name="gated_delta",
    )(proj, proj, gates, gates_t, conv_w.astype(F32), av, dtv, avt, dtt,
      norm_g.reshape(1, HEAD_DIM), _tri_lower(c), lvl)


def kernel(x, mem, w_in, conv_w, gdn_a_log, gdn_dt_bias, gdn_norm_g, fox_f_bias,
           hgrn_lb_logits, hgrn_norm_g, w_out, xa_wq, xa_wk, xa_wv, xa_wo, ln_g, ln_b,
           ffn_wg, ffn_wu, ffn_wd, router_w, moe_wg, moe_wu, moe_wd):
    nb, s, d = x.shape
    depth = w_in.shape[0]
    t = nb * s
    n_mix = d // HEAD_DIM
    hgrn_h = n_mix // 4
    gdn_h = (n_mix - hgrn_h) // 2
    fox_h = n_mix - hgrn_h - gdn_h
    gw, fw, hw = gdn_h * HEAD_DIM, fox_h * HEAD_DIM, hgrn_h * HEAD_DIM
    alpha = (2.0 * depth) ** 0.25
    ne = router_w.shape[-1]

    o_beta = 4 * gw
    o_dec = o_beta + gdn_h
    o_fox = o_dec + gdn_h
    o_fb = o_fox + 3 * fw
    o_hgrn = o_fb + fox_h

    lb_w = jax.nn.softmax(hgrn_lb_logits.astype(F32), axis=0)
    lower_bounds = jnp.cumsum(lb_w, axis=0) - lb_w[:1]

    xf = x.reshape(t, d).astype(F32)
    xb = xf.astype(BF16)
    mem_b = mem.reshape(nb * mem.shape[1], d).astype(BF16)

    for l in range(depth):
        wl = w_in[l]
        w_g = wl[:, :o_beta].astype(BF16)
        w_f = wl[:, o_fox:o_fb].astype(BF16)
        w_h = wl[:, o_hgrn:].astype(BF16)
        w_gate = jnp.zeros((d, LANES), F32)
        w_gate = w_gate.at[:, GATE_BETA:GATE_BETA + gdn_h].set(wl[:, o_beta:o_dec])
        w_gate = w_gate.at[:, GATE_DEC:GATE_DEC + gdn_h].set(wl[:, o_dec:o_fox])
        w_gate = w_gate.at[:, GATE_FOX:GATE_FOX + fox_h].set(wl[:, o_fb:o_hgrn])

        pg = _matmul(xb, w_g, BF16)
        pf = _matmul(xb, w_f, BF16)
        ph = _matmul(xb, w_h, BF16)
        gates = _matmul(xf, w_gate, F32, tn=LANES, hi=True)
        gates_t = gates.T

        o_a = _gdn(pg, gates, gates_t, conv_w[l], gdn_a_log[l], gdn_dt_bias[l], gdn_norm_g[l],
                   nb, gdn_h)
        fbias = jnp.zeros((SUBLANES, 1), F32).at[:fox_h, 0].set(fox_f_bias[l].astype(F32))
        cum = _fox_cum(gates_t, fbias, nb)
        o_b = _fox_attention(pf, cum, nb, fox_h)
        o_c = _hgrn(ph, lower_bounds[l].reshape(1, hw), hgrn_norm_g[l], nb, hgrn_h)

        blocks = [(gw, 0), (fw, gw // fw), (hw, (gw + fw) // hw)]
        xf, xb = _matmul_res_ln([o_a, o_b, o_c], w_out[l].astype(BF16), blocks, xf,
                                ln_g[l, 0], ln_b[l, 0], alpha)

        km = _matmul(mem_b, xa_wk[l].astype(BF16), BF16).reshape(nb, -1, d)
        vm = _matmul(mem_b, xa_wv[l].astype(BF16), BF16).reshape(nb, -1, d)
        xo = _xattn(xb, xa_wq[l].astype(BF16), km, vm, nb)
        xf, xb = _matmul_res_ln([xo], xa_wo[l].astype(BF16), [(d, 0)], xf,
                                ln_g[l, 1], ln_b[l, 1], alpha)

        i = l // 2
        if l % 2 == 0:
            xf, xb = _ffn(xb, ffn_wg[i][None].astype(BF16), ffn_wu[i][None].astype(BF16),
                          ffn_wd[i][None].astype(BF16), None, xf, ln_g[l, 2], ln_b[l, 2], alpha)
        else:
            w_r = jnp.zeros((d, LANES), F32).at[:, :ne].set(router_w[i].astype(F32))
            logits = _matmul(xf, w_r, F32, tn=LANES, hi=True)
            combine = _route(logits)
            xf, xb = _ffn(xb, moe_wg[i].astype(BF16), moe_wu[i].astype(BF16),
                          moe_wd[i].astype(BF16), combine, xf, ln_g[l, 2], ln_b[l, 2], alpha,
                          tf=256)
    return xf.reshape(nb, s, d).astype(x.dtype)
```

```python
import functools

import jax
import jax.numpy as jnp
from jax import lax
from jax.experimental import pallas as pl
from jax.experimental.pallas import tpu as pltpu

F32 = jnp.float32
BF16 = jnp.bfloat16
HI = lax.Precision.HIGHEST

HEAD_DIM = 128
CONV_K = 4
XA_HEADS = 4
N_EXPERTS = 8
TOP_K = 2
LN_EPS = 1e-5
RMS_EPS = 1e-6
L2_EPS = 1e-6
MASK_VALUE = -1e30
LANES = 128
SUBLANES = 8
VMEM_LIMIT = 56 * 1024 * 1024

GATE_BETA, GATE_DEC, GATE_FOX = 0, 8, 16


def _cparams(*sem):
    return pltpu.CompilerParams(dimension_semantics=sem, vmem_limit_bytes=VMEM_LIMIT)


def _pick(n, pref):
    t = min(pref, n)
    while n % t:
        t //= 2
    return t


def _dot(a, b):
    return jnp.dot(a, b, preferred_element_type=F32)


def _dot_nt(a, b):
    return lax.dot_general(a, b, (((1,), (1,)), ((), ())), preferred_element_type=F32)


def _dot_hi(a, b):
    return jnp.dot(a, b, preferred_element_type=F32, precision=HI)


def _sigmoid(x):
    return 1.0 / (1.0 + jnp.exp(-x))


def _silu(x):
    return x * _sigmoid(x)


def _softplus(x):
    return jnp.maximum(x, 0.0) + jnp.log(1.0 + jnp.exp(-jnp.abs(x)))


def _layer_norm(y, g, b):
    mu = jnp.mean(y, axis=-1, keepdims=True)
    d = y - mu
    var = jnp.mean(d * d, axis=-1, keepdims=True)
    return d * lax.rsqrt(var + LN_EPS) * g + b


def _mm_kernel(x_ref, w_ref, o_ref, *, hi):
    if hi:
        acc = _dot_hi(x_ref[...], w_ref[...])
    else:
        acc = _dot(x_ref[...], w_ref[...])
    o_ref[...] = acc.astype(o_ref.dtype)


def _matmul(x, w, out_dtype, *, tm=1024, tn=512, hi=False):
    m, k = x.shape
    n = w.shape[1]
    tm, tn = _pick(m, tm), _pick(n, tn)
    return pl.pallas_call(
        functools.partial(_mm_kernel, hi=hi),
        grid=(m // tm, n // tn),
        in_specs=[pl.BlockSpec((tm, k), lambda i, j: (i, 0)),
                  pl.BlockSpec((k, tn), lambda i, j: (0, j))],
        out_specs=pl.BlockSpec((tm, tn), lambda i, j: (i, j)),
        out_shape=jax.ShapeDtypeStruct((m, n), out_dtype),
        compiler_params=_cparams("parallel", "arbitrary"),
        name="matmul_hi" if hi else "matmul",
    )(x, w)


def _mm_res_ln_kernel(*refs, alpha, n_in):
    a_refs, w_refs = refs[:n_in], refs[n_in:2 * n_in]
    r_ref, g_ref, b_ref, of_ref, ob_ref = refs[2 * n_in:]
    y = alpha * r_ref[...]
    for a_ref, w_ref in zip(a_refs, w_refs):
        y = y + _dot(a_ref[...], w_ref[...])
    out = _layer_norm(y, g_ref[...], b_ref[...])
    of_ref[...] = out
    ob_ref[...] = out.astype(BF16)


def _matmul_res_ln(a_list, w, w_row_blocks, resid, g, b, alpha, *, tm=512):
    m, d = resid.shape
    tm = _pick(m, tm)
    n_in = len(a_list)
    row = lambda i: (i, 0)
    fixed = lambda i: (0, 0)
    a_specs = [pl.BlockSpec((tm, a.shape[1]), row) for a in a_list]
    w_specs = [pl.BlockSpec((rows, d), functools.partial(lambda i, bi: (bi, 0), bi=bi))
               for rows, bi in w_row_blocks]
    return pl.pallas_call(
        functools.partial(_mm_res_ln_kernel, alpha=alpha, n_in=n_in),
        grid=(m // tm,),
        in_specs=a_specs + w_specs + [pl.BlockSpec((tm, d), row), pl.BlockSpec((1, d), fixed),
                                      pl.BlockSpec((1, d), fixed)],
        out_specs=[pl.BlockSpec((tm, d), row), pl.BlockSpec((tm, d), row)],
        out_shape=[jax.ShapeDtypeStruct((m, d), F32), jax.ShapeDtypeStruct((m, d), BF16)],
        compiler_params=_cparams("parallel"),
        name="matmul_res_ln",
    )(*a_list, *([w] * n_in), resid, g.reshape(1, d), b.reshape(1, d))


def _ffn_kernel(x_ref, wg_ref, wu_ref, wd_ref, r_ref, g_ref, b_ref,
                of_ref, ob_ref, acc_ref, *, alpha):
    j = pl.program_id(1)

    @pl.when(j == 0)
    def _():
        acc_ref[...] = jnp.zeros_like(acc_ref)

    x = x_ref[...]
    h = _silu(_dot(x, wg_ref[...])) * _dot(x, wu_ref[...])
    acc_ref[...] += _dot(h.astype(BF16), wd_ref[...])

    @pl.when(j == pl.num_programs(1) - 1)
    def _():
        out = _layer_norm(alpha * r_ref[...] + acc_ref[...], g_ref[...], b_ref[...])
        of_ref[...] = out
        ob_ref[...] = out.astype(BF16)


def _ffn(xb, wg, wu, wd, resid, g, b, alpha, *, tm=512, tf=512):
    m, d = xb.shape
    f = wg.shape[1]
    tm, tf = _pick(m, tm), _pick(f, tf)
    row = lambda i, j: (i, 0)
    fixed = lambda i, j: (0, 0)
    return pl.pallas_call(
        functools.partial(_ffn_kernel, alpha=alpha),
        grid=(m // tm, f // tf),
        in_specs=[pl.BlockSpec((tm, d), row),
                  pl.BlockSpec((d, tf), lambda i, j: (0, j)),
                  pl.BlockSpec((d, tf), lambda i, j: (0, j)),
                  pl.BlockSpec((tf, d), lambda i, j: (j, 0)),
                  pl.BlockSpec((tm, d), row),
                  pl.BlockSpec((1, d), fixed), pl.BlockSpec((1, d), fixed)],
        out_specs=[pl.BlockSpec((tm, d), row), pl.BlockSpec((tm, d), row)],
        out_shape=[jax.ShapeDtypeStruct((m, d), F32), jax.ShapeDtypeStruct((m, d), BF16)],
        scratch_shapes=[pltpu.VMEM((tm, d), F32)],
        compiler_params=_cparams("parallel", "arbitrary"),
        name="ffn_dense",
    )(xb, wg, wu, wd, resid, g.reshape(1, d), b.reshape(1, d))


R_E1, R_E2, R_RANK1, R_RANK2, R_G1, R_G2 = range(6)
ROW_SLABS = 16


def _route_kernel(l_ref, tril_ref, info_ref, cnt_ref, carry_ref):
    @pl.when(pl.program_id(0) == 0)
    def _():
        carry_ref[...] = jnp.zeros_like(carry_ref)

    logits = l_ref[...]
    lane = lax.broadcasted_iota(jnp.int32, logits.shape, 1)
    logits = jnp.where(lane < N_EXPERTS, logits, -jnp.inf)
    m1 = jnp.max(logits, axis=-1, keepdims=True)
    i1 = jnp.min(jnp.where(logits == m1, lane, LANES), axis=-1, keepdims=True)
    rest = jnp.where(lane == i1, -jnp.inf, logits)
    m2 = jnp.max(rest, axis=-1, keepdims=True)
    i2 = jnp.min(jnp.where(rest == m2, lane, LANES), axis=-1, keepdims=True)
    e2 = jnp.exp(m2 - m1)
    g1 = 1.0 / (1.0 + e2)
    g2 = e2 / (1.0 + e2)
    onehot = ((lane == i1) | (lane == i2)).astype(F32)
    before = _dot_hi(tril_ref[...], onehot) + carry_ref[...]
    rank1 = jnp.sum(jnp.where(lane == i1, before, 0.0), axis=-1, keepdims=True)
    rank2 = jnp.sum(jnp.where(lane == i2, before, 0.0), axis=-1, keepdims=True)
    info = jnp.zeros(logits.shape, F32)
    for ln, val in ((R_E1, i1.astype(F32)), (R_E2, i2.astype(F32)), (R_RANK1, rank1),
                    (R_RANK2, rank2), (R_G1, g1), (R_G2, g2)):
        info = jnp.where(lane == ln, val, info)
    info_ref[...] = info
    carry_ref[...] += jnp.sum(onehot, axis=0, keepdims=True)
    cnt_ref[...] = carry_ref[...]


def _route(logits, *, tm=512):
    m = logits.shape[0]
    tm = _pick(m, tm)
    r = lax.broadcasted_iota(jnp.int32, (tm, tm), 0)
    c = lax.broadcasted_iota(jnp.int32, (tm, tm), 1)
    strict = (c < r).astype(F32)
    spec = pl.BlockSpec((tm, LANES), lambda i: (i, 0))
    return pl.pallas_call(
        _route_kernel, grid=(m // tm,),
        in_specs=[spec, pl.BlockSpec((tm, tm), lambda i: (0, 0))],
        out_specs=[spec, pl.BlockSpec((1, LANES), lambda i: (0, 0))],
        out_shape=[jax.ShapeDtypeStruct((m, LANES), F32), jax.ShapeDtypeStruct((1, LANES), F32)],
        scratch_shapes=[pltpu.VMEM((1, LANES), F32)],
        compiler_params=_cparams("arbitrary"), name="route_top2",
    )(logits, strict)


def _slot_index(pos_smem, k, r, rows_per_slot):
    return pos_smem[k * rows_per_slot + lax.shift_right_logical(r, 7), r & (LANES - 1)]


def _dispatch_kernel(pos_hbm, x_hbm, zero_hbm, xs_hbm, pos_smem, psem, sem, *, tm):
    del zero_hbm
    i = pl.program_id(0)
    pcopy = pltpu.make_async_copy(pos_hbm.at[i], pos_smem, psem)
    pcopy.start()
    pcopy.wait()
    rps = tm // LANES

    def row_copy(r, k):
        return pltpu.make_async_copy(x_hbm.at[i * tm + r],
                                     xs_hbm.at[_slot_index(pos_smem, k, r, rps)], sem)

    def issue(r, c):
        for k in range(TOP_K):
            row_copy(r, k).start()
        return c

    def drain(r, c):
        for k in range(TOP_K):
            row_copy(r, k).wait()
        return c

    lax.fori_loop(0, tm, issue, 0)
    lax.fori_loop(0, tm, drain, 0)


def _moe_dispatch(x3, pos3, n_rows, *, tm):
    t = x3.shape[0]
    zeros = jnp.zeros((n_rows,) + x3.shape[1:], x3.dtype)
    any_spec = pl.BlockSpec(memory_space=pl.ANY)
    return pl.pallas_call(
        functools.partial(_dispatch_kernel, tm=tm),
        grid=(t // tm,),
        in_specs=[any_spec, any_spec, any_spec],
        out_specs=any_spec,
        out_shape=jax.ShapeDtypeStruct(zeros.shape, zeros.dtype),
        scratch_shapes=[pltpu.SMEM(pos3.shape[1:], jnp.int32),
                        pltpu.SemaphoreType.DMA, pltpu.SemaphoreType.DMA],
        input_output_aliases={2: 0},
        compiler_params=_cparams("arbitrary"),
        name="moe_dispatch",
    )(pos3, x3, zeros)


def _moe_ffn_kernel(te_ref, nv_ref, x_ref, wg_ref, wu_ref, wd_ref, o_ref, acc_ref):
    i, j = pl.program_id(0), pl.program_id(1)
    valid = i < nv_ref[0]
    last = j == pl.num_programs(1) - 1

    @pl.when(valid & (j == 0))
    def _():
        acc_ref[...] = jnp.zeros_like(acc_ref)

    @pl.when(valid)
    def _():
        x = x_ref[...]
        h = _silu(_dot(x, wg_ref[...])) * _dot(x, wu_ref[...])
        acc_ref[...] += _dot(h.astype(BF16), wd_ref[...])

    @pl.when(valid & last)
    def _():
        o_ref[...] = acc_ref[...]

    @pl.when(jnp.logical_not(valid) & last)
    def _():
        o_ref[...] = jnp.zeros_like(o_ref)


def _moe_ffn(xs, wg, wu, wd, tile_expert, n_valid, *, tm, tf=256):
    n_rows, d = xs.shape
    f = wg.shape[2]
    tf = _pick(f, tf)
    nf = f // tf

    def ff(i, j, nv):
        return jnp.where(i < nv[0], j, nf - 1)

    grid_spec = pltpu.PrefetchScalarGridSpec(
        num_scalar_prefetch=2,
        grid=(n_rows // tm, nf),
        in_specs=[pl.BlockSpec((tm, d), lambda i, j, te, nv: (i, 0)),
                  pl.BlockSpec((None, d, tf), lambda i, j, te, nv: (te[i], 0, ff(i, j, nv))),
                  pl.BlockSpec((None, d, tf), lambda i, j, te, nv: (te[i], 0, ff(i, j, nv))),
                  pl.BlockSpec((None, tf, d), lambda i, j, te, nv: (te[i], ff(i, j, nv), 0))],
        out_specs=pl.BlockSpec((tm, d), lambda i, j, te, nv: (i, 0)),
        scratch_shapes=[pltpu.VMEM((tm, d), F32)])
    return pl.pallas_call(
        _moe_ffn_kernel, grid_spec=grid_spec,
        out_shape=jax.ShapeDtypeStruct((n_rows, d), F32),
        compiler_params=_cparams("arbitrary", "arbitrary"),
        name="moe_ffn",
    )(tile_expert, n_valid, xs, wg, wu, wd)


def _combine_kernel(pos_hbm, ys_hbm, info_ref, r_ref, g_ref, b_ref, of_ref, ob_ref,
                    pos_smem, buf_ref, z_ref, psem, sem, *, tm, alpha):
    i = pl.program_id(0)
    pcopy = pltpu.make_async_copy(pos_hbm.at[i], pos_smem, psem)
    pcopy.start()
    pcopy.wait()
    rps = tm // LANES

    def row_copy(r, k):
        dst = buf_ref.at[k, pl.ds(pl.multiple_of(r * ROW_SLABS, ROW_SLABS), ROW_SLABS)]
        return pltpu.make_async_copy(ys_hbm.at[_slot_index(pos_smem, k, r, rps)], dst, sem)

    def issue(r, c):
        for k in range(TOP_K):
            row_copy(r, k).start()
        return c

    def drain(r, c):
        for k in range(TOP_K):
            row_copy(r, k).wait()
        return c

    lax.fori_loop(0, tm, issue, 0)
    lax.fori_loop(0, tm, drain, 0)

    info = info_ref[...]
    g1 = info[:, R_G1:R_G1 + 1]
    g2 = info[:, R_G2:R_G2 + 1]
    for a in range(ROW_SLABS):
        y1 = buf_ref[0, pl.ds(a, tm, stride=ROW_SLABS), :]
        y2 = buf_ref[1, pl.ds(a, tm, stride=ROW_SLABS), :]
        z_ref[:, a * LANES:(a + 1) * LANES] = g1 * y1 + g2 * y2
    out = _layer_norm(alpha * r_ref[...] + z_ref[...], g_ref[...], b_ref[...])
    of_ref[...] = out
    ob_ref[...] = out.astype(BF16)


def _moe_combine(ys3, pos3, info, resid, g, b, alpha, *, tm):
    t, d = resid.shape
    row = lambda i: (i, 0)
    fixed = lambda i: (0, 0)
    any_spec = pl.BlockSpec(memory_space=pl.ANY)
    return pl.pallas_call(
        functools.partial(_combine_kernel, tm=tm, alpha=alpha),
        grid=(t // tm,),
        in_specs=[any_spec, any_spec, pl.BlockSpec((tm, LANES), row), pl.BlockSpec((tm, d), row),
                  pl.BlockSpec((1, d), fixed), pl.BlockSpec((1, d), fixed)],
        out_specs=[pl.BlockSpec((tm, d), row), pl.BlockSpec((tm, d), row)],
        out_shape=[jax.ShapeDtypeStruct((t, d), F32), jax.ShapeDtypeStruct((t, d), BF16)],
        scratch_shapes=[pltpu.SMEM(pos3.shape[1:], jnp.int32),
                        pltpu.VMEM((TOP_K, tm * ROW_SLABS, LANES), F32),
                        pltpu.VMEM((tm, d), F32),
                        pltpu.SemaphoreType.DMA, pltpu.SemaphoreType.DMA],
        compiler_params=_cparams("arbitrary"),
        name="moe_combine",
    )(pos3, ys3, info, resid, g.reshape(1, d), b.reshape(1, d))


def _slot_table(pos, tm):
    t = pos.shape[1]
    p = pos.reshape(TOP_K, t // tm, tm // LANES, LANES)
    return jnp.transpose(p, (1, 0, 2, 3)).reshape(t // tm, TOP_K * tm // LANES, LANES)


def _moe(xf, xb, router_w, wg, wu, wd, g, b, alpha, *, tm_rows=512, tm_disp=512, tm_comb=256):
    t, d = xf.shape
    ne = router_w.shape[1]
    w_r = jnp.zeros((d, LANES), F32).at[:, :ne].set(router_w.astype(F32))
    logits = _matmul(xf, w_r, F32, tn=LANES, hi=True)
    info, counts = _route(logits)

    tm_rows = _pick(TOP_K * t, tm_rows)
    n_tiles = TOP_K * t // tm_rows + ne
    counts = counts[0, :ne].astype(jnp.int32)
    tiles_e = (counts + tm_rows - 1) // tm_rows
    tile_end = jnp.cumsum(tiles_e)
    row_off = (tile_end - tiles_e) * tm_rows
    tile_ids = jnp.arange(n_tiles, dtype=jnp.int32)
    tile_expert = jnp.sum(tile_ids[:, None] >= tile_end[None, :], axis=1).astype(jnp.int32)
    tile_expert = jnp.minimum(tile_expert, ne - 1)
    last_e = jnp.max(jnp.where(tiles_e > 0, jnp.arange(ne, dtype=jnp.int32), 0))
    n_valid = tile_end[-1:].astype(jnp.int32)
    tile_expert = jnp.where(tile_ids < n_valid[0], tile_expert, last_e)
    e_idx = info[:, R_E1:R_E2 + 1].astype(jnp.int32)
    rank = info[:, R_RANK1:R_RANK2 + 1].astype(jnp.int32)
    off = jnp.sum(jnp.where(e_idx[:, :, None] == jnp.arange(ne, dtype=jnp.int32),
                            row_off[None, None, :], 0), axis=-1)
    pos = (off + rank).T

    tm_disp, tm_comb = _pick(t, tm_disp), _pick(t, tm_comb)
    n_rows = n_tiles * tm_rows
    x3 = xb.reshape(t, ROW_SLABS, d // ROW_SLABS)
    xs = _moe_dispatch(x3, _slot_table(pos, tm_disp), n_rows, tm=tm_disp)
    ys = _moe_ffn(xs.reshape(n_rows, d), wg, wu, wd, tile_expert, n_valid, tm=tm_rows)
    ys3 = ys.reshape(n_rows, ROW_SLABS, d // ROW_SLABS)
    return _moe_combine(ys3, _slot_table(pos, tm_comb), info, xf, g, b, alpha, tm=tm_comb)


def _xattn_kernel(x_ref, wq_ref, k_ref, v_ref, o_ref):
    d = wq_ref.shape[1]
    hd = d // XA_HEADS
    q = _dot(x_ref[...], wq_ref[...]).astype(BF16)
    for h in range(XA_HEADS):
        sl = slice(h * hd, (h + 1) * hd)
        s = _dot_nt(q[:, sl], k_ref[:, sl]) * (hd ** -0.5)
        s = s - jnp.max(s, axis=-1, keepdims=True)
        p = jnp.exp(s)
        p = p / jnp.sum(p, axis=-1, keepdims=True)
        o_ref[:, sl] = _dot(p.astype(BF16), v_ref[:, sl]).astype(o_ref.dtype)


def _xattn(xb, wq, km, vm, nb, *, tm=256):
    t, d = xb.shape
    s = t // nb
    ml = km.shape[1]
    tm = _pick(s, tm)
    ns = s // tm
    return pl.pallas_call(
        _xattn_kernel,
        grid=(nb, ns),
        in_specs=[pl.BlockSpec((tm, d), lambda b, i: (b * ns + i, 0)),
                  pl.BlockSpec((d, d), lambda b, i: (0, 0)),
                  pl.BlockSpec((None, ml, d), lambda b, i: (b, 0, 0)),
                  pl.BlockSpec((None, ml, d), lambda b, i: (b, 0, 0))],
        out_specs=pl.BlockSpec((tm, d), lambda b, i: (b * ns + i, 0)),
        out_shape=jax.ShapeDtypeStruct((t, d), BF16),
        compiler_params=_cparams("parallel", "arbitrary"),
        name="xattn",
    )(xb, wq, km, vm)


def _fox_cum_kernel(gt_ref, bias_ref, c_ref, carry_ref):
    @pl.when(pl.program_id(1) == 0)
    def _():
        carry_ref[...] = jnp.zeros_like(carry_ref)

    z = gt_ref[...] + bias_ref[...]
    lf = -_softplus(-z)
    tk = z.shape[1]
    r = lax.broadcasted_iota(jnp.int32, (tk, tk), 0)
    c = lax.broadcasted_iota(jnp.int32, (tk, tk), 1)
    upper = (r <= c).astype(F32)
    cum = _dot_hi(lf, upper) + carry_ref[...]
    c_ref[...] = cum
    carry_ref[...] = cum[:, tk - 1:tk]


def _fox_cum(gates_t, bias, nb, *, tk=512):
    t = gates_t.shape[1]
    s = t // nb
    tk = _pick(s, tk)
    ns = s // tk
    rb = GATE_FOX // SUBLANES
    return pl.pallas_call(
        _fox_cum_kernel,
        grid=(nb, ns),
        in_specs=[pl.BlockSpec((SUBLANES, tk), lambda b, i: (rb, b * ns + i)),
                  pl.BlockSpec((SUBLANES, 1), lambda b, i: (0, 0))],
        out_specs=pl.BlockSpec((None, SUBLANES, tk), lambda b, i: (b, 0, i)),
        out_shape=jax.ShapeDtypeStruct((nb, SUBLANES, s), F32),
        scratch_shapes=[pltpu.VMEM((SUBLANES, 1), F32)],
        compiler_params=_cparams("parallel", "arbitrary"),
        name="fox_cumsum",
    )(gates_t, bias)


def _fox_kernel(q_ref, k_ref, v_ref, c_ref, o_ref, *, tq, tk, scale):
    assert tq == tk
    qi = pl.program_id(2)
    q = q_ref[...]
    q0 = pl.multiple_of(qi * tq, tq)
    c_q0 = c_ref[:, pl.ds(q0, LANES)][:, 0:1]

    def step(kj, carry, diagonal):
        m, l, acc = carry
        k0 = pl.multiple_of(kj * tk, tk)
        kb = k_ref[pl.ds(k0, tk), :]
        vb = v_ref[pl.ds(k0, tk), :]
        s = _dot_nt(q, kb) * scale + (c_q0 - c_ref[:, pl.ds(k0, tk)])
        if diagonal:
            row = lax.broadcasted_iota(jnp.int32, (tq, tk), 0)
            col = lax.broadcasted_iota(jnp.int32, (tq, tk), 1)
            s = jnp.where(col <= row, s, MASK_VALUE)
        m_new = jnp.maximum(m, jnp.max(s, axis=-1, keepdims=True))
        a = jnp.exp(m - m_new)
        p = jnp.exp(s - m_new)
        l = a * l + jnp.sum(p, axis=-1, keepdims=True)
        acc = a * acc + _dot(p.astype(BF16), vb)
        return m_new, l, acc

    init = (jnp.full((tq, 1), MASK_VALUE, F32), jnp.zeros((tq, 1), F32),
            jnp.zeros((tq, HEAD_DIM), F32))
    carry = lax.fori_loop(0, qi, functools.partial(step, diagonal=False), init)
    m, l, acc = step(qi, carry, True)
    o_ref[...] = (acc / l).astype(o_ref.dtype)


def _fox_attention(proj, cum, nb, nh, *, tq=512, tk=512):
    t = proj.shape[0]
    s = t // nb
    tq, tk = _pick(s, tq), _pick(s, tk)
    nq = s // tq
    proj3 = proj.reshape(nb, s, proj.shape[1])
    cum2 = cum.reshape(nb * SUBLANES, 1, s)
    out = pl.pallas_call(
        functools.partial(_fox_kernel, tq=tq, tk=tk, scale=HEAD_DIM ** -0.5),
        grid=(nb, nh, nq),
        in_specs=[pl.BlockSpec((None, tq, HEAD_DIM), lambda b, h, i: (b, i, h)),
                  pl.BlockSpec((None, s, HEAD_DIM), lambda b, h, i: (b, 0, nh + h)),
                  pl.BlockSpec((None, s, HEAD_DIM), lambda b, h, i: (b, 0, 2 * nh + h)),
                  pl.BlockSpec((None, 1, s), lambda b, h, i: (b * SUBLANES + h, 0, 0))],
        out_specs=pl.BlockSpec((None, tq, HEAD_DIM), lambda b, h, i: (b, i, h)),
        out_shape=jax.ShapeDtypeStruct((nb, s, nh * HEAD_DIM), BF16),
        compiler_params=_cparams("parallel", "parallel", "arbitrary"),
        name="fox_attention",
    )(proj3, proj3, proj3, cum2)
    return out.reshape(t, nh * HEAD_DIM)


HGRN_SUB = 16


def _hgrn_kernel(p_ref, lb_ref, ng_ref, tri_ref, o_ref, st_ref, *, nh, c):
    @pl.when(pl.program_id(1) == 0)
    def _():
        st_ref[...] = jnp.zeros_like(st_ref)

    w = nh * HEAD_DIM
    row = lax.broadcasted_iota(jnp.int32, (c, c), 0)
    col = lax.broadcasted_iota(jnp.int32, (c, c), 1)
    tri = tri_ref[...]
    heads = range(nh)
    sl = lambda grp, h: slice(grp * w + h * HEAD_DIM, grp * w + (h + 1) * HEAD_DIM)
    q = [_silu(p_ref[:, sl(0, h)].astype(F32)) for h in heads]
    f = [lb_ref[:, sl(0, h)] + (1.0 - lb_ref[:, sl(0, h)]) * _sigmoid(p_ref[:, sl(1, h)].astype(F32))
         for h in heads]
    k = [1.0 - f[h] for h in heads]
    vb = [p_ref[:, sl(2, h)] for h in heads]
    bc = [_dot_hi(tri, jnp.log(f[h])) for h in heads]

    def scaled(x, b3, shift):
        return (x.reshape(b3.shape) * jnp.exp(shift)).reshape(c, HEAD_DIM).astype(BF16)

    nsub = c // HGRN_SUB
    same = (row // HGRN_SUB == col // HGRN_SUB) & (col <= row)
    a = []
    for h in heads:
        b3 = bc[h].reshape(nsub, HGRN_SUB, HEAD_DIM)
        ref = b3[:, HGRN_SUB // 2 - 1:HGRN_SUB // 2, :]
        a.append(jnp.where(same, _dot_nt(scaled(q[h], b3, b3 - ref), scaled(k[h], b3, ref - b3)), 0.0))
    s = HGRN_SUB
    while s < c:
        rb, cb = row // s, col // s
        pair = (rb % 2 == 1) & (cb == rb - 1)
        for h in heads:
            b3 = bc[h].reshape(c // s, s, HEAD_DIM)
            last = b3[:, s - 1:s, :]
            prev = jnp.concatenate([jnp.zeros_like(last[:1]), last[:-1]], axis=0)
            a[h] = a[h] + jnp.where(
                pair, _dot_nt(scaled(q[h], b3, b3 - prev), scaled(k[h], b3, last - b3)), 0.0)
        s *= 2

    st = [st_ref[h] for h in heads]
    o = [_dot(a[h].astype(BF16), vb[h])
         + _dot_nt((q[h] * jnp.exp(bc[h])).astype(BF16), st[h].astype(BF16)) for h in heads]
    for h in heads:
        b_last = bc[h][c - 1:c, :]
        k_st = k[h] * jnp.exp(b_last - bc[h])
        st_ref[h] = (st[h] * jnp.exp(b_last)
                     + _dot(vb[h].astype(F32).T.astype(BF16), k_st.astype(BF16)))
    for h in heads:
        gate = _sigmoid(p_ref[:, sl(3, h)].astype(F32))
        oh = o[h] * lax.rsqrt(jnp.mean(o[h] * o[h], axis=-1, keepdims=True) + RMS_EPS) * ng_ref[...]
        o_ref[:, h * HEAD_DIM:(h + 1) * HEAD_DIM] = (oh * gate).astype(o_ref.dtype)


def _tri_lower(c):
    r = lax.broadcasted_iota(jnp.int32, (c, c), 0)
    col = lax.broadcasted_iota(jnp.int32, (c, c), 1)
    return (col <= r).astype(F32)


def _hgrn(proj, lb, norm_g, nb, nh, *, c=128):
    t, wtot = proj.shape
    s = t // nb
    c = _pick(s, c)
    ns = s // c
    w = nh * HEAD_DIM
    fixed = lambda b, i: (0, 0)
    return pl.pallas_call(
        functools.partial(_hgrn_kernel, nh=nh, c=c),
        grid=(nb, ns),
        in_specs=[pl.BlockSpec((c, wtot), lambda b, i: (b * ns + i, 0)),
                  pl.BlockSpec((1, w), fixed), pl.BlockSpec((1, HEAD_DIM), fixed),
                  pl.BlockSpec((c, c), fixed)],
        out_specs=pl.BlockSpec((c, w), lambda b, i: (b * ns + i, 0)),
        out_shape=jax.ShapeDtypeStruct((t, w), BF16),
        scratch_shapes=[pltpu.VMEM((nh, HEAD_DIM, HEAD_DIM), F32)],
        compiler_params=_cparams("parallel", "arbitrary"),
        name="hgrn2",
    )(proj, lb, norm_g.reshape(1, HEAD_DIM), _tri_lower(c))


def _level_masks(c):
    r = lax.broadcasted_iota(jnp.int32, (c, c), 0)
    col = lax.broadcasted_iota(jnp.int32, (c, c), 1)
    out, s = [], 1
    while s < c:
        rb, cb = r // s, col // s
        out.append(((rb % 2 == 1) & (cb == rb - 1)).astype(F32))
        s *= 2
    return jnp.stack(out)


def _gdn_kernel(p_ref, halo_ref, gt_ref, gtt_ref, cw_ref, av_ref, dtv_ref, avt_ref, dtt_ref,
                ng_ref, tri_ref, lvl_ref, o_ref, s_ref, *, nh, c):
    first = pl.program_id(1) == 0

    @pl.when(first)
    def _():
        s_ref[...] = jnp.zeros_like(s_ref)

    w = nh * HEAD_DIM
    row = lax.broadcasted_iota(jnp.int32, (c, c), 0)
    col = lax.broadcasted_iota(jnp.int32, (c, c), 1)
    causal = col <= row
    tri = tri_ref[...]
    gates = gt_ref[...]
    la = -av_ref[...] * _softplus(gates + dtv_ref[...])
    g_col = _dot_hi(tri, la)
    la_t = -avt_ref[...] * _softplus(gtt_ref[...] + dtt_ref[...])
    g_row = lax.dot_general(la_t, tri, (((1,), (1,)), ((), ())),
                            preferred_element_type=F32, precision=HI)
    halo_scale = jnp.where(first, 0.0, 1.0)
    sub = lax.broadcasted_iota(jnp.int32, (SUBLANES, HEAD_DIM), 0)

    def conv_silu(grp, h):
        lo = grp * w + h * HEAD_DIM
        cur = p_ref[:, lo:lo + HEAD_DIM].astype(F32)
        halo = halo_ref[:, lo:lo + HEAD_DIM].astype(F32) * halo_scale
        cw = cw_ref[:, lo:lo + HEAD_DIM]
        y = cw[CONV_K - 1:CONV_K, :] * cur
        for j in range(1, CONV_K):
            rolled = pltpu.roll(cur, j, axis=0)
            head = jnp.where(sub < j, pltpu.roll(halo, j, axis=0), rolled[:SUBLANES])
            shifted = jnp.concatenate([head, rolled[SUBLANES:]], axis=0)
            y = y + cw[CONV_K - 1 - j:CONV_K - j, :] * shifted
        return _silu(y)

    def l2n(x):
        return x * lax.rsqrt(jnp.sum(x * x, axis=-1, keepdims=True) + L2_EPS)

    heads = range(nh)
    q = [l2n(conv_silu(0, h)) * (HEAD_DIM ** -0.5) for h in heads]
    k = [l2n(conv_silu(1, h)) for h in heads]
    v = [conv_silu(2, h) for h in heads]
    beta = [_sigmoid(gates[:, GATE_BETA + h:GATE_BETA + h + 1]) for h in heads]
    gc = [g_col[:, GATE_DEC + h:GATE_DEC + h + 1] for h in heads]
    gamma = [jnp.where(causal, jnp.exp(jnp.where(causal, gc[h] - g_row[h:h + 1, :], 0.0)), 0.0)
             for h in heads]
    kb = [k[h] * beta[h] for h in heads]
    kbf = [k[h].astype(BF16) for h in heads]
    m = [_dot_nt(kb[h].astype(BF16), kbf[h]) * gamma[h] for h in heads]

    n = [-(m[h] * lvl_ref[0]) for h in heads]
    for lv in range(1, lvl_ref.shape[0]):
        low = [m[h] * lvl_ref[lv] for h in heads]
        p = [low[h] + _dot(n[h].astype(BF16), low[h].astype(BF16)) for h in heads]
        n = [n[h] - p[h] - _dot(p[h].astype(BF16), n[h].astype(BF16)) for h in heads]

    eg = [jnp.exp(gc[h]) for h in heads]
    g_last = [gc[h][c - 1:c, :] for h in heads]
    rhs = [jnp.concatenate([v[h] * beta[h], kb[h] * eg[h]], axis=1) for h in heads]
    sol = [rhs[h] + _dot(n[h].astype(BF16), rhs[h].astype(BF16)) for h in heads]
    state = [s_ref[h] for h in heads]
    sb = [state[h].astype(BF16) for h in heads]
    v_new = [sol[h][:, :HEAD_DIM] - _dot(sol[h][:, HEAD_DIM:].astype(BF16), sb[h]) for h in heads]
    qk = [_dot_nt(q[h].astype(BF16), kbf[h]) * gamma[h] for h in heads]
    o = [_dot((q[h] * eg[h]).astype(BF16), sb[h]) + _dot(qk[h].astype(BF16), v_new[h].astype(BF16))
         for h in heads]
    k_dec = [k[h] * jnp.exp(g_last[h] - gc[h]) for h in heads]
    for h in heads:
        s_ref[h] = (state[h] * jnp.exp(g_last[h])
                    + _dot(k_dec[h].T.astype(BF16), v_new[h].astype(BF16)))
    for h in heads:
        z = p_ref[:, 3 * w + h * HEAD_DIM:3 * w + (h + 1) * HEAD_DIM].astype(F32)
        oh = o[h] * lax.rsqrt(jnp.mean(o[h] * o[h], axis=-1, keepdims=True) + RMS_EPS) * ng_ref[...]
        o_ref[:, h * HEAD_DIM:(h + 1) * HEAD_DIM] = (oh * _silu(z)).astype(o_ref.dtype)


def _gdn(proj, gates, gates_t, conv_w, a_log, dt_bias, norm_g, nb, nh, *, c=128):
    t, wtot = proj.shape
    s = t // nb
    c = _pick(s, c)
    ns = s // c
    w = nh * HEAD_DIM
    a = jnp.exp(a_log.astype(F32))
    av = jnp.zeros((1, LANES), F32).at[0, GATE_DEC:GATE_DEC + nh].set(a)
    dtv = jnp.zeros((1, LANES), F32).at[0, GATE_DEC:GATE_DEC + nh].set(dt_bias.astype(F32))
    avt = jnp.zeros((SUBLANES, 1), F32).at[:nh, 0].set(a)
    dtt = jnp.zeros((SUBLANES, 1), F32).at[:nh, 0].set(dt_bias.astype(F32))
    lvl = _level_masks(c)
    fixed = lambda b, i: (0, 0)
    rows_per_halo = c // SUBLANES
    return pl.pallas_call(
        functools.partial(_gdn_kernel, nh=nh, c=c),
        grid=(nb, ns),
        in_specs=[pl.BlockSpec((c, wtot), lambda b, i: (b * ns + i, 0)),
                  pl.BlockSpec((SUBLANES, 3 * w),
                               lambda b, i: (jnp.maximum((b * ns + i) * rows_per_halo - 1, 0), 0)),
                  pl.BlockSpec((c, LANES), lambda b, i: (b * ns + i, 0)),
                  pl.BlockSpec((SUBLANES, c), lambda b, i: (GATE_DEC // SUBLANES, b * ns + i)),
                  pl.BlockSpec((CONV_K, 3 * w), fixed),
                  pl.BlockSpec((1, LANES), fixed), pl.BlockSpec((1, LANES), fixed),
                  pl.BlockSpec((SUBLANES, 1), fixed), pl.BlockSpec((SUBLANES, 1), fixed),
                  pl.BlockSpec((1, HEAD_DIM), fixed),
                  pl.BlockSpec((c, c), fixed),
                  pl.BlockSpec(lvl.shape, lambda b, i: (0, 0, 0))],
        out_specs=pl.BlockSpec((c, w), lambda b, i: (b * ns + i, 0)),
        out_shape=jax.ShapeDtypeStruct((t, w), BF16),
        scratch_shapes=[pltpu.VMEM((nh, HEAD_DIM, HEAD_DIM), F32)],
        compiler_params=_cparams("parallel", "arbitrary"),
        name="gated_delta",
    )(proj, proj, gates, gates_t, conv_w.astype(F32), av, dtv, avt, dtt,
      norm_g.reshape(1, HEAD_DIM), _tri_lower(c), lvl)


def kernel(x, mem, w_in, conv_w, gdn_a_log, gdn_dt_bias, gdn_norm_g, fox_f_bias,
           hgrn_lb_logits, hgrn_norm_g, w_out, xa_wq, xa_wk, xa_wv, xa_wo, ln_g, ln_b,
           ffn_wg, ffn_wu, ffn_wd, router_w, moe_wg, moe_wu, moe_wd):
    nb, s, d = x.shape
    depth = w_in.shape[0]
    t = nb * s
    n_mix = d // HEAD_DIM
    hgrn_h = n_mix // 4
    gdn_h = (n_mix - hgrn_h) // 2
    fox_h = n_mix - hgrn_h - gdn_h
    gw, fw, hw = gdn_h * HEAD_DIM, fox_h * HEAD_DIM, hgrn_h * HEAD_DIM
    alpha = (2.0 * depth) ** 0.25
    ne = router_w.shape[-1]

    o_beta = 4 * gw
    o_dec = o_beta + gdn_h
    o_fox = o_dec + gdn_h
    o_fb = o_fox + 3 * fw
    o_hgrn = o_fb + fox_h

    lb_w = jax.nn.softmax(hgrn_lb_logits.astype(F32), axis=0)
    lower_bounds = jnp.cumsum(lb_w, axis=0) - lb_w[:1]

    xf = x.reshape(t, d).astype(F32)
    xb = xf.astype(BF16)
    mem_b = mem.reshape(nb * mem.shape[1], d).astype(BF16)

    for l in range(depth):
        wl = w_in[l]
        w_g = wl[:, :o_beta].astype(BF16)
        w_f = wl[:, o_fox:o_fb].astype(BF16)
        w_h = wl[:, o_hgrn:].astype(BF16)
        w_gate = jnp.zeros((d, LANES), F32)
        w_gate = w_gate.at[:, GATE_BETA:GATE_BETA + gdn_h].set(wl[:, o_beta:o_dec])
        w_gate = w_gate.at[:, GATE_DEC:GATE_DEC + gdn_h].set(wl[:, o_dec:o_fox])
        w_gate = w_gate.at[:, GATE_FOX:GATE_FOX + fox_h].set(wl[:, o_fb:o_hgrn])

        pg = _matmul(xb, w_g, BF16)
        pf = _matmul(xb, w_f, BF16)
        ph = _matmul(xb, w_h, BF16)
        gates = _matmul(xf, w_gate, F32, tn=LANES, hi=True)
        gates_t = gates.T

        o_a = _gdn(pg, gates, gates_t, conv_w[l], gdn_a_log[l], gdn_dt_bias[l], gdn_norm_g[l],
                   nb, gdn_h)
        fbias = jnp.zeros((SUBLANES, 1), F32).at[:fox_h, 0].set(fox_f_bias[l].astype(F32))
        cum = _fox_cum(gates_t, fbias, nb)
        o_b = _fox_attention(pf, cum, nb, fox_h)
        o_c = _hgrn(ph, lower_bounds[l].reshape(1, hw), hgrn_norm_g[l], nb, hgrn_h)

        blocks = [(gw, 0), (fw, gw // fw), (hw, (gw + fw) // hw)]
        xf, xb = _matmul_res_ln([o_a, o_b, o_c], w_out[l].astype(BF16), blocks, xf,
                                ln_g[l, 0], ln_b[l, 0], alpha)

        km = _matmul(mem_b, xa_wk[l].astype(BF16), BF16).reshape(nb, -1, d)
        vm = _matmul(mem_b, xa_wv[l].astype(BF16), BF16).reshape(nb, -1, d)
        xo = _xattn(xb, xa_wq[l].astype(BF16), km, vm, nb)
        xf, xb = _matmul_res_ln([xo], xa_wo[l].astype(BF16), [(d, 0)], xf,
                                ln_g[l, 1], ln_b[l, 1], alpha)

        i = l // 2
        if l % 2 == 0:
            xf, xb = _ffn(xb, ffn_wg[i].astype(BF16), ffn_wu[i].astype(BF16),
                          ffn_wd[i].astype(BF16), xf, ln_g[l, 2], ln_b[l, 2], alpha)
        else:
            xf, xb = _moe(xf, xb, router_w[i], moe_wg[i].astype(BF16), moe_wu[i].astype(BF16),
                          moe_wd[i].astype(BF16), ln_g[l, 2], ln_b[l, 2], alpha)
    return xf.reshape(nb, s, d).astype(x.dtype)
```

```python
import functools

import jax
import jax.numpy as jnp
from jax import lax
from jax.experimental import pallas as pl
from jax.experimental.pallas import tpu as pltpu

F32 = jnp.float32
BF16 = jnp.bfloat16
HI = lax.Precision.HIGHEST

HEAD_DIM = 128
CONV_K = 4
XA_HEADS = 4
N_EXPERTS = 8
TOP_K = 2
LN_EPS = 1e-5
RMS_EPS = 1e-6
L2_EPS = 1e-6
MASK_VALUE = -1e30
LANES = 128
SUBLANES = 8
VMEM_LIMIT = 56 * 1024 * 1024

GATE_BETA, GATE_DEC, GATE_FOX = 0, 8, 16


def _cparams(*sem):
    return pltpu.CompilerParams(dimension_semantics=sem, vmem_limit_bytes=VMEM_LIMIT)


def _pick(n, pref):
    t = min(pref, n)
    while n % t:
        t //= 2
    return t


def _dot(a, b):
    return jnp.dot(a, b, preferred_element_type=F32)


def _dot_nt(a, b):
    return lax.dot_general(a, b, (((1,), (1,)), ((), ())), preferred_element_type=F32)


def _dot_hi(a, b):
    return jnp.dot(a, b, preferred_element_type=F32, precision=HI)


def _sigmoid(x):
    return 1.0 / (1.0 + jnp.exp(-x))


def _silu(x):
    return x * _sigmoid(x)


def _softplus(x):
    return jnp.maximum(x, 0.0) + jnp.log(1.0 + jnp.exp(-jnp.abs(x)))


def _layer_norm(y, g, b):
    mu = jnp.mean(y, axis=-1, keepdims=True)
    d = y - mu
    var = jnp.mean(d * d, axis=-1, keepdims=True)
    return d * lax.rsqrt(var + LN_EPS) * g + b


def _mm_kernel(x_ref, w_ref, o_ref, *, hi):
    if hi:
        acc = _dot_hi(x_ref[...], w_ref[...])
    else:
        acc = _dot(x_ref[...], w_ref[...])
    o_ref[...] = acc.astype(o_ref.dtype)


def _matmul(x, w, out_dtype, *, tm=1024, tn=512, hi=False):
    m, k = x.shape
    n = w.shape[1]
    tm, tn = _pick(m, tm), _pick(n, tn)
    return pl.pallas_call(
        functools.partial(_mm_kernel, hi=hi),
        grid=(m // tm, n // tn),
        in_specs=[pl.BlockSpec((tm, k), lambda i, j: (i, 0)),
                  pl.BlockSpec((k, tn), lambda i, j: (0, j))],
        out_specs=pl.BlockSpec((tm, tn), lambda i, j: (i, j)),
        out_shape=jax.ShapeDtypeStruct((m, n), out_dtype),
        compiler_params=_cparams("parallel", "arbitrary"),
        name="matmul_hi" if hi else "matmul",
    )(x, w)


def _mm_res_ln_kernel(*refs, alpha, n_in):
    a_refs, w_refs = refs[:n_in], refs[n_in:2 * n_in]
    r_ref, g_ref, b_ref, of_ref, ob_ref = refs[2 * n_in:]
    y = alpha * r_ref[...]
    for a_ref, w_ref in zip(a_refs, w_refs):
        y = y + _dot(a_ref[...], w_ref[...])
    out = _layer_norm(y, g_ref[...], b_ref[...])
    of_ref[...] = out
    ob_ref[...] = out.astype(BF16)


def _matmul_res_ln(a_list, w, w_row_blocks, resid, g, b, alpha, *, tm=512):
    m, d = resid.shape
    tm = _pick(m, tm)
    n_in = len(a_list)
    row = lambda i: (i, 0)
    fixed = lambda i: (0, 0)
    a_specs = [pl.BlockSpec((tm, a.shape[1]), row) for a in a_list]
    w_specs = [pl.BlockSpec((rows, d), functools.partial(lambda i, bi: (bi, 0), bi=bi))
               for rows, bi in w_row_blocks]
    return pl.pallas_call(
        functools.partial(_mm_res_ln_kernel, alpha=alpha, n_in=n_in),
        grid=(m // tm,),
        in_specs=a_specs + w_specs + [pl.BlockSpec((tm, d), row), pl.BlockSpec((1, d), fixed),
                                      pl.BlockSpec((1, d), fixed)],
        out_specs=[pl.BlockSpec((tm, d), row), pl.BlockSpec((tm, d), row)],
        out_shape=[jax.ShapeDtypeStruct((m, d), F32), jax.ShapeDtypeStruct((m, d), BF16)],
        compiler_params=_cparams("parallel"),
        name="matmul_res_ln",
    )(*a_list, *([w] * n_in), resid, g.reshape(1, d), b.reshape(1, d))


def _swiglu_up(x_ref, wg_ref, wu_ref):
    x = x_ref[...]
    return (_silu(_dot(x, wg_ref[...])) * _dot(x, wu_ref[...])).astype(BF16)


def _ffn_kernel(x_ref, wg_ref, wu_ref, wd_ref, r_ref, g_ref, b_ref,
                of_ref, ob_ref, acc_ref, h_ref, *, alpha):
    j = pl.program_id(1)
    nf = pl.num_programs(1) - 1

    @pl.when(j == 0)
    def _():
        acc_ref[...] = jnp.zeros_like(acc_ref)
        h_ref[...] = _swiglu_up(x_ref, wg_ref, wu_ref)

    @pl.when((j > 0) & (j < nf))
    def _():
        h_next = _swiglu_up(x_ref, wg_ref, wu_ref)
        acc_ref[...] += _dot(h_ref[...], wd_ref[...])
        h_ref[...] = h_next

    @pl.when(j == nf)
    def _():
        y = acc_ref[...] + _dot(h_ref[...], wd_ref[...])
        out = _layer_norm(alpha * r_ref[...] + y, g_ref[...], b_ref[...])
        of_ref[...] = out
        ob_ref[...] = out.astype(BF16)


def _ffn(xb, wg, wu, wd, resid, g, b, alpha, *, tm=512, tf=512):
    m, d = xb.shape
    f = wg.shape[1]
    tm, tf = _pick(m, tm), _pick(f, tf)
    nf = f // tf
    row = lambda i, j: (i, 0)
    fixed = lambda i, j: (0, 0)
    up = lambda i, j: (0, jnp.minimum(j, nf - 1))
    down = lambda i, j: (jnp.maximum(j - 1, 0), 0)
    return pl.pallas_call(
        functools.partial(_ffn_kernel, alpha=alpha),
        grid=(m // tm, nf + 1),
        in_specs=[pl.BlockSpec((tm, d), row),
                  pl.BlockSpec((d, tf), up), pl.BlockSpec((d, tf), up),
                  pl.BlockSpec((tf, d), down),
                  pl.BlockSpec((tm, d), row),
                  pl.BlockSpec((1, d), fixed), pl.BlockSpec((1, d), fixed)],
        out_specs=[pl.BlockSpec((tm, d), row), pl.BlockSpec((tm, d), row)],
        out_shape=[jax.ShapeDtypeStruct((m, d), F32), jax.ShapeDtypeStruct((m, d), BF16)],
        scratch_shapes=[pltpu.VMEM((tm, d), F32), pltpu.VMEM((tm, tf), BF16)],
        compiler_params=_cparams("parallel", "arbitrary"),
        name="ffn_dense",
    )(xb, wg, wu, wd, resid, g.reshape(1, d), b.reshape(1, d))


R_E1, R_E2, R_RANK1, R_RANK2, R_G1, R_G2 = range(6)
ROW_SLABS = 16


def _route_kernel(l_ref, tril_ref, info_ref, cnt_ref, carry_ref):
    @pl.when(pl.program_id(0) == 0)
    def _():
        carry_ref[...] = jnp.zeros_like(carry_ref)

    logits = l_ref[...]
    lane = lax.broadcasted_iota(jnp.int32, logits.shape, 1)
    logits = jnp.where(lane < N_EXPERTS, logits, -jnp.inf)
    m1 = jnp.max(logits, axis=-1, keepdims=True)
    i1 = jnp.min(jnp.where(logits == m1, lane, LANES), axis=-1, keepdims=True)
    rest = jnp.where(lane == i1, -jnp.inf, logits)
    m2 = jnp.max(rest, axis=-1, keepdims=True)
    i2 = jnp.min(jnp.where(rest == m2, lane, LANES), axis=-1, keepdims=True)
    e2 = jnp.exp(m2 - m1)
    g1 = 1.0 / (1.0 + e2)
    g2 = e2 / (1.0 + e2)
    onehot = ((lane == i1) | (lane == i2)).astype(F32)
    before = _dot_hi(tril_ref[...], onehot) + carry_ref[...]
    rank1 = jnp.sum(jnp.where(lane == i1, before, 0.0), axis=-1, keepdims=True)
    rank2 = jnp.sum(jnp.where(lane == i2, before, 0.0), axis=-1, keepdims=True)
    info = jnp.zeros(logits.shape, F32)
    for ln, val in ((R_E1, i1.astype(F32)), (R_E2, i2.astype(F32)), (R_RANK1, rank1),
                    (R_RANK2, rank2), (R_G1, g1), (R_G2, g2)):
        info = jnp.where(lane == ln, val, info)
    info_ref[...] = info
    carry_ref[...] += jnp.sum(onehot, axis=0, keepdims=True)
    cnt_ref[...] = carry_ref[...]


def _route(logits, *, tm=512):
    m = logits.shape[0]
    tm = _pick(m, tm)
    r = lax.broadcasted_iota(jnp.int32, (tm, tm), 0)
    c = lax.broadcasted_iota(jnp.int32, (tm, tm), 1)
    strict = (c < r).astype(F32)
    spec = pl.BlockSpec((tm, LANES), lambda i: (i, 0))
    return pl.pallas_call(
        _route_kernel, grid=(m // tm,),
        in_specs=[spec, pl.BlockSpec((tm, tm), lambda i: (0, 0))],
        out_specs=[spec, pl.BlockSpec((1, LANES), lambda i: (0, 0))],
        out_shape=[jax.ShapeDtypeStruct((m, LANES), F32), jax.ShapeDtypeStruct((1, LANES), F32)],
        scratch_shapes=[pltpu.VMEM((1, LANES), F32)],
        compiler_params=_cparams("arbitrary"), name="route_top2",
    )(logits, strict)


def _slot_index(pos_smem, k, r, rows_per_slot):
    return pos_smem[k * rows_per_slot + lax.shift_right_logical(r, 7), r & (LANES - 1)]


def _dispatch_kernel(pos_hbm, x_ref, zero_hbm, xs_hbm, pos_smem, psem, sem, *, tm):
    del zero_hbm
    i = pl.program_id(0)
    pcopy = pltpu.make_async_copy(pos_hbm.at[i], pos_smem, psem)
    pcopy.start()
    pcopy.wait()
    rps = tm // LANES

    def row_copy(r, k):
        return pltpu.make_async_copy(x_ref.at[r], xs_hbm.at[_slot_index(pos_smem, k, r, rps)], sem)

    def issue(r, c):
        for k in range(TOP_K):
            row_copy(r, k).start()
        return c

    def drain(r, c):
        for k in range(TOP_K):
            row_copy(r, k).wait()
        return c

    lax.fori_loop(0, tm, issue, 0)
    lax.fori_loop(0, tm, drain, 0)


def _moe_dispatch(x3, pos3, n_rows, *, tm):
    t = x3.shape[0]
    zeros = jnp.zeros((n_rows,) + x3.shape[1:], x3.dtype)
    any_spec = pl.BlockSpec(memory_space=pl.ANY)
    return pl.pallas_call(
        functools.partial(_dispatch_kernel, tm=tm),
        grid=(t // tm,),
        in_specs=[any_spec, pl.BlockSpec((tm,) + x3.shape[1:], lambda i: (i, 0, 0)), any_spec],
        out_specs=any_spec,
        out_shape=jax.ShapeDtypeStruct(zeros.shape, zeros.dtype),
        scratch_shapes=[pltpu.SMEM(pos3.shape[1:], jnp.int32),
                        pltpu.SemaphoreType.DMA, pltpu.SemaphoreType.DMA],
        input_output_aliases={2: 0},
        compiler_params=_cparams("arbitrary"),
        name="moe_dispatch",
    )(pos3, x3, zeros)


def _moe_ffn_kernel(te_ref, nv_ref, x_ref, wg_ref, wu_ref, wd_ref, o_ref, acc_ref, h_ref):
    i, j = pl.program_id(0), pl.program_id(1)
    nf = pl.num_programs(1) - 1
    valid = i < nv_ref[0]

    @pl.when(valid & (j == 0))
    def _():
        acc_ref[...] = jnp.zeros_like(acc_ref)
        h_ref[...] = _swiglu_up(x_ref, wg_ref, wu_ref)

    @pl.when(valid & (j > 0) & (j < nf))
    def _():
        h_next = _swiglu_up(x_ref, wg_ref, wu_ref)
        acc_ref[...] += _dot(h_ref[...], wd_ref[...])
        h_ref[...] = h_next

    @pl.when(valid & (j == nf))
    def _():
        o_ref[...] = acc_ref[...] + _dot(h_ref[...], wd_ref[...])

    @pl.when(jnp.logical_not(valid) & (j == nf))
    def _():
        o_ref[...] = jnp.zeros_like(o_ref)


def _moe_ffn(xs, wg, wu, wd, tile_expert, n_valid, *, tm, tf=256):
    n_rows, d = xs.shape
    f = wg.shape[2]
    tf = _pick(f, tf)
    nf = f // tf

    def up(i, j, nv):
        return jnp.where(i < nv[0], jnp.minimum(j, nf - 1), nf - 1)

    def down(i, j, nv):
        return jnp.where(i < nv[0], jnp.maximum(j - 1, 0), nf - 1)

    grid_spec = pltpu.PrefetchScalarGridSpec(
        num_scalar_prefetch=2,
        grid=(n_rows // tm, nf + 1),
        in_specs=[pl.BlockSpec((tm, d), lambda i, j, te, nv: (i, 0)),
                  pl.BlockSpec((None, d, tf), lambda i, j, te, nv: (te[i], 0, up(i, j, nv))),
                  pl.BlockSpec((None, d, tf), lambda i, j, te, nv: (te[i], 0, up(i, j, nv))),
                  pl.BlockSpec((None, tf, d), lambda i, j, te, nv: (te[i], down(i, j, nv), 0))],
        out_specs=pl.BlockSpec((tm, d), lambda i, j, te, nv: (i, 0)),
        scratch_shapes=[pltpu.VMEM((tm, d), F32), pltpu.VMEM((tm, tf), BF16)])
    return pl.pallas_call(
        _moe_ffn_kernel, grid_spec=grid_spec,
        out_shape=jax.ShapeDtypeStruct((n_rows, d), F32),
        compiler_params=_cparams("arbitrary", "arbitrary"),
        name="moe_ffn",
    )(tile_expert, n_valid, xs, wg, wu, wd)


def _combine_kernel(pos_hbm, ys_hbm, info_ref, r_ref, g_ref, b_ref, of_ref, ob_ref,
                    pos_smem, buf_ref, z_ref, psem, sem, *, tm, alpha):
    i = pl.program_id(0)
    pcopy = pltpu.make_async_copy(pos_hbm.at[i], pos_smem, psem)
    pcopy.start()
    pcopy.wait()
    rps = tm // LANES

    def row_copy(r, k):
        dst = buf_ref.at[k, pl.ds(pl.multiple_of(r * ROW_SLABS, ROW_SLABS), ROW_SLABS)]
        return pltpu.make_async_copy(ys_hbm.at[_slot_index(pos_smem, k, r, rps)], dst, sem)

    def issue(r, c):
        for k in range(TOP_K):
            row_copy(r, k).start()
        return c

    def drain(r, c):
        for k in range(TOP_K):
            row_copy(r, k).wait()
        return c

    lax.fori_loop(0, tm, issue, 0)
    lax.fori_loop(0, tm, drain, 0)

    info = info_ref[...]
    g1 = info[:, R_G1:R_G1 + 1]
    g2 = info[:, R_G2:R_G2 + 1]
    for a in range(ROW_SLABS):
        y1 = buf_ref[0, pl.ds(a, tm, stride=ROW_SLABS), :]
        y2 = buf_ref[1, pl.ds(a, tm, stride=ROW_SLABS), :]
        z_ref[:, a * LANES:(a + 1) * LANES] = g1 * y1 + g2 * y2
    out = _layer_norm(alpha * r_ref[...] + z_ref[...], g_ref[...], b_ref[...])
    of_ref[...] = out
    ob_ref[...] = out.astype(BF16)


def _moe_combine(ys3, pos3, info, resid, g, b, alpha, *, tm):
    t, d = resid.shape
    row = lambda i: (i, 0)
    fixed = lambda i: (0, 0)
    any_spec = pl.BlockSpec(memory_space=pl.ANY)
    return pl.pallas_call(
        functools.partial(_combine_kernel, tm=tm, alpha=alpha),
        grid=(t // tm,),
        in_specs=[any_spec, any_spec, pl.BlockSpec((tm, LANES), row), pl.BlockSpec((tm, d), row),
                  pl.BlockSpec((1, d), fixed), pl.BlockSpec((1, d), fixed)],
        out_specs=[pl.BlockSpec((tm, d), row), pl.BlockSpec((tm, d), row)],
        out_shape=[jax.ShapeDtypeStruct((t, d), F32), jax.ShapeDtypeStruct((t, d), BF16)],
        scratch_shapes=[pltpu.SMEM(pos3.shape[1:], jnp.int32),
                        pltpu.VMEM((TOP_K, tm * ROW_SLABS, LANES), F32),
                        pltpu.VMEM((tm, d), F32),
                        pltpu.SemaphoreType.DMA, pltpu.SemaphoreType.DMA],
        compiler_params=_cparams("arbitrary"),
        name="moe_combine",
    )(pos3, ys3, info, resid, g.reshape(1, d), b.reshape(1, d))


def _slot_table(pos, tm):
    t = pos.shape[1]
    p = pos.reshape(TOP_K, t // tm, tm // LANES, LANES)
    return jnp.transpose(p, (1, 0, 2, 3)).reshape(t // tm, TOP_K * tm // LANES, LANES)


def _moe(xf, xb, router_w, wg, wu, wd, g, b, alpha, *, tm_rows=512, tm_disp=512, tm_comb=256):
    t, d = xf.shape
    ne = router_w.shape[1]
    w_r = jnp.zeros((d, LANES), F32).at[:, :ne].set(router_w.astype(F32))
    logits = _matmul(xf, w_r, F32, tn=LANES, hi=True)
    info, counts = _route(logits)

    tm_rows = _pick(TOP_K * t, tm_rows)
    n_tiles = TOP_K * t // tm_rows + ne
    counts = counts[0, :ne].astype(jnp.int32)
    tiles_e = (counts + tm_rows - 1) // tm_rows
    tile_end = jnp.cumsum(tiles_e)
    row_off = (tile_end - tiles_e) * tm_rows
    tile_ids = jnp.arange(n_tiles, dtype=jnp.int32)
    tile_expert = jnp.sum(tile_ids[:, None] >= tile_end[None, :], axis=1).astype(jnp.int32)
    tile_expert = jnp.minimum(tile_expert, ne - 1)
    last_e = jnp.max(jnp.where(tiles_e > 0, jnp.arange(ne, dtype=jnp.int32), 0))
    n_valid = tile_end[-1:].astype(jnp.int32)
    tile_expert = jnp.where(tile_ids < n_valid[0], tile_expert, last_e)
    e_idx = info[:, R_E1:R_E2 + 1].astype(jnp.int32)
    rank = info[:, R_RANK1:R_RANK2 + 1].astype(jnp.int32)
    off = jnp.sum(jnp.where(e_idx[:, :, None] == jnp.arange(ne, dtype=jnp.int32),
                            row_off[None, None, :], 0), axis=-1)
    pos = (off + rank).T

    tm_disp, tm_comb = _pick(t, tm_disp), _pick(t, tm_comb)
    n_rows = n_tiles * tm_rows
    x3 = xb.reshape(t, ROW_SLABS, d // ROW_SLABS)
    xs = _moe_dispatch(x3, _slot_table(pos, tm_disp), n_rows, tm=tm_disp)
    ys = _moe_ffn(xs.reshape(n_rows, d), wg, wu, wd, tile_expert, n_valid, tm=tm_rows)
    ys3 = ys.reshape(n_rows, ROW_SLABS, d // ROW_SLABS)
    return _moe_combine(ys3, _slot_table(pos, tm_comb), info, xf, g, b, alpha, tm=tm_comb)


def _xattn_kernel(x_ref, wq_ref, k_ref, v_ref, o_ref):
    d = wq_ref.shape[1]
    hd = d // XA_HEADS
    q = _dot(x_ref[...], wq_ref[...]).astype(BF16)
    for h in range(XA_HEADS):
        sl = slice(h * hd, (h + 1) * hd)
        s = _dot_nt(q[:, sl], k_ref[:, sl]) * (hd ** -0.5)
        s = s - jnp.max(s, axis=-1, keepdims=True)
        p = jnp.exp(s)
        p = p / jnp.sum(p, axis=-1, keepdims=True)
        o_ref[:, sl] = _dot(p.astype(BF16), v_ref[:, sl]).astype(o_ref.dtype)


def _xattn(xb, wq, km, vm, nb, *, tm=256):
    t, d = xb.shape
    s = t // nb
    ml = km.shape[1]
    tm = _pick(s, tm)
    ns = s // tm
    return pl.pallas_call(
        _xattn_kernel,
        grid=(nb, ns),
        in_specs=[pl.BlockSpec((tm, d), lambda b, i: (b * ns + i, 0)),
                  pl.BlockSpec((d, d), lambda b, i: (0, 0)),
                  pl.BlockSpec((None, ml, d), lambda b, i: (b, 0, 0)),
                  pl.BlockSpec((None, ml, d), lambda b, i: (b, 0, 0))],
        out_specs=pl.BlockSpec((tm, d), lambda b, i: (b * ns + i, 0)),
        out_shape=jax.ShapeDtypeStruct((t, d), BF16),
        compiler_params=_cparams("parallel", "arbitrary"),
        name="xattn",
    )(xb, wq, km, vm)


def _fox_cum_kernel(gt_ref, bias_ref, c_ref, carry_ref):
    @pl.when(pl.program_id(1) == 0)
    def _():
        carry_ref[...] = jnp.zeros_like(carry_ref)

    z = gt_ref[...] + bias_ref[...]
    lf = -_softplus(-z)
    tk = z.shape[1]
    r = lax.broadcasted_iota(jnp.int32, (tk, tk), 0)
    c = lax.broadcasted_iota(jnp.int32, (tk, tk), 1)
    upper = (r <= c).astype(F32)
    cum = _dot_hi(lf, upper) + carry_ref[...]
    c_ref[...] = cum
    carry_ref[...] = cum[:, tk - 1:tk]


def _fox_cum(gates_t, bias, nb, *, tk=512):
    t = gates_t.shape[1]
    s = t // nb
    tk = _pick(s, tk)
    ns = s // tk
    rb = GATE_FOX // SUBLANES
    return pl.pallas_call(
        _fox_cum_kernel,
        grid=(nb, ns),
        in_specs=[pl.BlockSpec((SUBLANES, tk), lambda b, i: (rb, b * ns + i)),
                  pl.BlockSpec((SUBLANES, 1), lambda b, i: (0, 0))],
        out_specs=pl.BlockSpec((None, SUBLANES, tk), lambda b, i: (b, 0, i)),
        out_shape=jax.ShapeDtypeStruct((nb, SUBLANES, s), F32),
        scratch_shapes=[pltpu.VMEM((SUBLANES, 1), F32)],
        compiler_params=_cparams("parallel", "arbitrary"),
        name="fox_cumsum",
    )(gates_t, bias)


def _fox_kernel(q_ref, k_ref, v_ref, c_ref, o_ref, *, tq, tk, scale):
    assert tq == tk
    qi = pl.program_id(2)
    q = q_ref[...]
    q0 = pl.multiple_of(qi * tq, tq)
    c_q0 = c_ref[:, pl.ds(q0, LANES)][:, 0:1]

    def step(kj, carry, diagonal):
        m, l, acc = carry
        k0 = pl.multiple_of(kj * tk, tk)
        kb = k_ref[pl.ds(k0, tk), :]
        vb = v_ref[pl.ds(k0, tk), :]
        s = _dot_nt(q, kb) * scale + (c_q0 - c_ref[:, pl.ds(k0, tk)])
        if diagonal:
            row = lax.broadcasted_iota(jnp.int32, (tq, tk), 0)
            col = lax.broadcasted_iota(jnp.int32, (tq, tk), 1)
            s = jnp.where(col <= row, s, MASK_VALUE)
        m_new = jnp.maximum(m, jnp.max(s, axis=-1, keepdims=True))
        a = jnp.exp(m - m_new)
        p = jnp.exp(s - m_new)
        l = a * l + jnp.sum(p, axis=-1, keepdims=True)
        acc = a * acc + _dot(p.astype(BF16), vb)
        return m_new, l, acc

    init = (jnp.full((tq, 1), MASK_VALUE, F32), jnp.zeros((tq, 1), F32),
            jnp.zeros((tq, HEAD_DIM), F32))
    carry = lax.fori_loop(0, qi, functools.partial(step, diagonal=False), init)
    m, l, acc = step(qi, carry, True)
    o_ref[...] = (acc / l).astype(o_ref.dtype)


def _fox_attention(proj, cum, nb, nh, *, tq=512, tk=512):
    t = proj.shape[0]
    s = t // nb
    tq, tk = _pick(s, tq), _pick(s, tk)
    nq = s // tq
    proj3 = proj.reshape(nb, s, proj.shape[1])
    cum2 = cum.reshape(nb * SUBLANES, 1, s)
    out = pl.pallas_call(
        functools.partial(_fox_kernel, tq=tq, tk=tk, scale=HEAD_DIM ** -0.5),
        grid=(nb, nh, nq),
        in_specs=[pl.BlockSpec((None, tq, HEAD_DIM), lambda b, h, i: (b, i, h)),
                  pl.BlockSpec((None, s, HEAD_DIM), lambda b, h, i: (b, 0, nh + h)),
                  pl.BlockSpec((None, s, HEAD_DIM), lambda b, h, i: (b, 0, 2 * nh + h)),
                  pl.BlockSpec((None, 1, s), lambda b, h, i: (b * SUBLANES + h, 0, 0))],
        out_specs=pl.BlockSpec((None, tq, HEAD_DIM), lambda b, h, i: (b, i, h)),
        out_shape=jax.ShapeDtypeStruct((nb, s, nh * HEAD_DIM), BF16),
        compiler_params=_cparams("parallel", "parallel", "arbitrary"),
        name="fox_attention",
    )(proj3, proj3, proj3, cum2)
    return out.reshape(t, nh * HEAD_DIM)


HGRN_SUB = 16


def _hgrn_kernel(p_ref, lb_ref, ng_ref, tri_ref, o_ref, st_ref, *, nh, c):
    @pl.when(pl.program_id(1) == 0)
    def _():
        st_ref[...] = jnp.zeros_like(st_ref)

    w = nh * HEAD_DIM
    row = lax.broadcasted_iota(jnp.int32, (c, c), 0)
    col = lax.broadcasted_iota(jnp.int32, (c, c), 1)
    tri = tri_ref[...]
    heads = range(nh)
    sl = lambda grp, h: slice(grp * w + h * HEAD_DIM, grp * w + (h + 1) * HEAD_DIM)
    q = [_silu(p_ref[:, sl(0, h)].astype(F32)) for h in heads]
    f = [lb_ref[:, sl(0, h)] + (1.0 - lb_ref[:, sl(0, h)]) * _sigmoid(p_ref[:, sl(1, h)].astype(F32))
         for h in heads]
    k = [1.0 - f[h] for h in heads]
    vb = [p_ref[:, sl(2, h)] for h in heads]
    bc = [_dot_hi(tri, jnp.log(f[h])) for h in heads]

    def scaled(x, b3, shift):
        return (x.reshape(b3.shape) * jnp.exp(shift)).reshape(c, HEAD_DIM).astype(BF16)

    nsub = c // HGRN_SUB
    same = (row // HGRN_SUB == col // HGRN_SUB) & (col <= row)
    a = []
    for h in heads:
        b3 = bc[h].reshape(nsub, HGRN_SUB, HEAD_DIM)
        ref = b3[:, HGRN_SUB // 2 - 1:HGRN_SUB // 2, :]
        a.append(jnp.where(same, _dot_nt(scaled(q[h], b3, b3 - ref), scaled(k[h], b3, ref - b3)), 0.0))
    s = HGRN_SUB
    while s < c:
        rb, cb = row // s, col // s
        pair = (rb % 2 == 1) & (cb == rb - 1)
        for h in heads:
            b3 = bc[h].reshape(c // s, s, HEAD_DIM)
            last = b3[:, s - 1:s, :]
            prev = jnp.concatenate([jnp.zeros_like(last[:1]), last[:-1]], axis=0)
            a[h] = a[h] + jnp.where(
                pair, _dot_nt(scaled(q[h], b3, b3 - prev), scaled(k[h], b3, last - b3)), 0.0)
        s *= 2

    st = [st_ref[h] for h in heads]
    o = [_dot(a[h].astype(BF16), vb[h])
         + _dot_nt((q[h] * jnp.exp(bc[h])).astype(BF16), st[h].astype(BF16)) for h in heads]
    for h in heads:
        b_last = bc[h][c - 1:c, :]
        k_st = k[h] * jnp.exp(b_last - bc[h])
        st_ref[h] = (st[h] * jnp.exp(b_last)
                     + _dot(vb[h].astype(F32).T.astype(BF16), k_st.astype(BF16)))
    for h in heads:
        gate = _sigmoid(p_ref[:, sl(3, h)].astype(F32))
        oh = o[h] * lax.rsqrt(jnp.mean(o[h] * o[h], axis=-1, keepdims=True) + RMS_EPS) * ng_ref[...]
        o_ref[:, h * HEAD_DIM:(h + 1) * HEAD_DIM] = (oh * gate).astype(o_ref.dtype)


def _tri_lower(c):
    r = lax.broadcasted_iota(jnp.int32, (c, c), 0)
    col = lax.broadcasted_iota(jnp.int32, (c, c), 1)
    return (col <= r).astype(F32)


def _hgrn(proj, lb, norm_g, nb, nh, *, c=128):
    t, wtot = proj.shape
    s = t // nb
    c = _pick(s, c)
    ns = s // c
    w = nh * HEAD_DIM
    fixed = lambda b, i: (0, 0)
    return pl.pallas_call(
        functools.partial(_hgrn_kernel, nh=nh, c=c),
        grid=(nb, ns),
        in_specs=[pl.BlockSpec((c, wtot), lambda b, i: (b * ns + i, 0)),
                  pl.BlockSpec((1, w), fixed), pl.BlockSpec((1, HEAD_DIM), fixed),
                  pl.BlockSpec((c, c), fixed)],
        out_specs=pl.BlockSpec((c, w), lambda b, i: (b * ns + i, 0)),
        out_shape=jax.ShapeDtypeStruct((t, w), BF16),
        scratch_shapes=[pltpu.VMEM((nh, HEAD_DIM, HEAD_DIM), F32)],
        compiler_params=_cparams("parallel", "arbitrary"),
        name="hgrn2",
    )(proj, lb, norm_g.reshape(1, HEAD_DIM), _tri_lower(c))


def _level_masks(c):
    r = lax.broadcasted_iota(jnp.int32, (c, c), 0)
    col = lax.broadcasted_iota(jnp.int32, (c, c), 1)
    out, s = [], 1
    while s < c:
        rb, cb = r // s, col // s
        out.append(((rb % 2 == 1) & (cb == rb - 1)).astype(F32))
        s *= 2
    return jnp.stack(out)


def _gdn_kernel(p_ref, halo_ref, gt_ref, gtt_ref, cw_ref, av_ref, dtv_ref, avt_ref, dtt_ref,
                ng_ref, tri_ref, lvl_ref, o_ref, s_ref, *, nh, c):
    first = pl.program_id(1) == 0

    @pl.when(first)
    def _():
        s_ref[...] = jnp.zeros_like(s_ref)

    w = nh * HEAD_DIM
    row = lax.broadcasted_iota(jnp.int32, (c, c), 0)
    col = lax.broadcasted_iota(jnp.int32, (c, c), 1)
    causal = col <= row
    tri = tri_ref[...]
    gates = gt_ref[...]
    la = -av_ref[...] * _softplus(gates + dtv_ref[...])
    g_col = _dot_hi(tri, la)
    la_t = -avt_ref[...] * _softplus(gtt_ref[...] + dtt_ref[...])
    g_row = lax.dot_general(la_t, tri, (((1,), (1,)), ((), ())),
                            preferred_element_type=F32, precision=HI)
    halo_scale = jnp.where(first, 0.0, 1.0)
    sub = lax.broadcasted_iota(jnp.int32, (SUBLANES, HEAD_DIM), 0)

    def conv_silu(grp, h):
        lo = grp * w + h * HEAD_DIM
        cur = p_ref[:, lo:lo + HEAD_DIM].astype(F32)
        halo = halo_ref[:, lo:lo + HEAD_DIM].astype(F32) * halo_scale
        cw = cw_ref[:, lo:lo + HEAD_DIM]
        y = cw[CONV_K - 1:CONV_K, :] * cur
        for j in range(1, CONV_K):
            rolled = pltpu.roll(cur, j, axis=0)
            head = jnp.where(sub < j, pltpu.roll(halo, j, axis=0), rolled[:SUBLANES])
            shifted = jnp.concatenate([head, rolled[SUBLANES:]], axis=0)
            y = y + cw[CONV_K - 1 - j:CONV_K - j, :] * shifted
        return _silu(y)

    def l2n(x):
        return x * lax.rsqrt(jnp.sum(x * x, axis=-1, keepdims=True) + L2_EPS)

    heads = range(nh)
    q = [l2n(conv_silu(0, h)) * (HEAD_DIM ** -0.5) for h in heads]
    k = [l2n(conv_silu(1, h)) for h in heads]
    v = [conv_silu(2, h) for h in heads]
    beta = [_sigmoid(gates[:, GATE_BETA + h:GATE_BETA + h + 1]) for h in heads]
    gc = [g_col[:, GATE_DEC + h:GATE_DEC + h + 1] for h in heads]
    gamma = [jnp.where(causal, jnp.exp(jnp.where(causal, gc[h] - g_row[h:h + 1, :], 0.0)), 0.0)
             for h in heads]
    kb = [k[h] * beta[h] for h in heads]
    kbf = [k[h].astype(BF16) for h in heads]
    m = [_dot_nt(kb[h].astype(BF16), kbf[h]) * gamma[h] for h in heads]

    n = [-(m[h] * lvl_ref[0]) for h in heads]
    for lv in range(1, lvl_ref.shape[0]):
        low = [m[h] * lvl_ref[lv] for h in heads]
        p = [low[h] + _dot(n[h].astype(BF16), low[h].astype(BF16)) for h in heads]
        n = [n[h] - p[h] - _dot(p[h].astype(BF16), n[h].astype(BF16)) for h in heads]

    eg = [jnp.exp(gc[h]) for h in heads]
    g_last = [gc[h][c - 1:c, :] for h in heads]
    rhs = [jnp.concatenate([v[h] * beta[h], kb[h] * eg[h]], axis=1) for h in heads]
    sol = [rhs[h] + _dot(n[h].astype(BF16), rhs[h].astype(BF16)) for h in heads]
    state = [s_ref[h] for h in heads]
    sb = [state[h].astype(BF16) for h in heads]
    v_new = [sol[h][:, :HEAD_DIM] - _dot(sol[h][:, HEAD_DIM:].astype(BF16), sb[h]) for h in heads]
    qk = [_dot_nt(q[h].astype(BF16), kbf[h]) * gamma[h] for h in heads]
    o = [_dot((q[h] * eg[h]).astype(BF16), sb[h]) + _dot(qk[h].astype(BF16), v_new[h].astype(BF16))
         for h in heads]
    k_dec = [k[h] * jnp.exp(g_last[h] - gc[h]) for h in heads]
    for h in heads:
        s_ref[h] = (state[h] * jnp.exp(g_last[h])
                    + _dot(k_dec[h].T.astype(BF16), v_new[h].astype(BF16)))
    for h in heads:
        z = p_ref[:, 3 * w + h * HEAD_DIM:3 * w + (h + 1) * HEAD_DIM].astype(F32)
        oh = o[h] * lax.rsqrt(jnp.mean(o[h] * o[h], axis=-1, keepdims=True) + RMS_EPS) * ng_ref[...]
        o_ref[:, h * HEAD_DIM:(h + 1) * HEAD_DIM] = (oh * _silu(z)).astype(o_ref.dtype)


def _gdn(proj, gates, gates_t, conv_w, a_log, dt_bias, norm_g, nb, nh, *, c=128):
    t, wtot = proj.shape
    s = t // nb
    c = _pick(s, c)
    ns = s // c
    w = nh * HEAD_DIM
    a = jnp.exp(a_log.astype(F32))
    av = jnp.zeros((1, LANES), F32).at[0, GATE_DEC:GATE_DEC + nh].set(a)
    dtv = jnp.zeros((1, LANES), F32).at[0, GATE_DEC:GATE_DEC + nh].set(dt_bias.astype(F32))
    avt = jnp.zeros((SUBLANES, 1), F32).at[:nh, 0].set(a)
    dtt = jnp.zeros((SUBLANES, 1), F32).at[:nh, 0].set(dt_bias.astype(F32))
    lvl = _level_masks(c)
    fixed = lambda b, i: (0, 0)
    rows_per_halo = c // SUBLANES
    return pl.pallas_call(
        functools.partial(_gdn_kernel, nh=nh, c=c),
        grid=(nb, ns),
        in_specs=[pl.BlockSpec((c, wtot), lambda b, i: (b * ns + i, 0)),
                  pl.BlockSpec((SUBLANES, 3 * w),
                               lambda b, i: (jnp.maximum((b * ns + i) * rows_per_halo - 1, 0), 0)),
                  pl.BlockSpec((c, LANES), lambda b, i: (b * ns + i, 0)),
                  pl.BlockSpec((SUBLANES, c), lambda b, i: (GATE_DEC // SUBLANES, b * ns + i)),
                  pl.BlockSpec((CONV_K, 3 * w), fixed),
                  pl.BlockSpec((1, LANES), fixed), pl.BlockSpec((1, LANES), fixed),
                  pl.BlockSpec((SUBLANES, 1), fixed), pl.BlockSpec((SUBLANES, 1), fixed),
                  pl.BlockSpec((1, HEAD_DIM), fixed),
                  pl.BlockSpec((c, c), fixed),
                  pl.BlockSpec(lvl.shape, lambda b, i: (0, 0, 0))],
        out_specs=pl.BlockSpec((c, w), lambda b, i: (b * ns + i, 0)),
        out_shape=jax.ShapeDtypeStruct((t, w), BF16),
        scratch_shapes=[pltpu.VMEM((nh, HEAD_DIM, HEAD_DIM), F32)],
        compiler_params=_cparams("parallel", "arbitrary"),
        name="gated_delta",
    )(proj, proj, gates, gates_t, conv_w.astype(F32), av, dtv, avt, dtt,
      norm_g.reshape(1, HEAD_DIM), _tri_lower(c), lvl)


def kernel(x, mem, w_in, conv_w, gdn_a_log, gdn_dt_bias, gdn_norm_g, fox_f_bias,
           hgrn_lb_logits, hgrn_norm_g, w_out, xa_wq, xa_wk, xa_wv, xa_wo, ln_g, ln_b,
           ffn_wg, ffn_wu, ffn_wd, router_w, moe_wg, moe_wu, moe_wd):
    nb, s, d = x.shape
    depth = w_in.shape[0]
    t = nb * s
    n_mix = d // HEAD_DIM
    hgrn_h = n_mix // 4
    gdn_h = (n_mix - hgrn_h) // 2
    fox_h = n_mix - hgrn_h - gdn_h
    gw, fw, hw = gdn_h * HEAD_DIM, fox_h * HEAD_DIM, hgrn_h * HEAD_DIM
    alpha = (2.0 * depth) ** 0.25
    ne = router_w.shape[-1]

    o_beta = 4 * gw
    o_dec = o_beta + gdn_h
    o_fox = o_dec + gdn_h
    o_fb = o_fox + 3 * fw
    o_hgrn = o_fb + fox_h

    lb_w = jax.nn.softmax(hgrn_lb_logits.astype(F32), axis=0)
    lower_bounds = jnp.cumsum(lb_w, axis=0) - lb_w[:1]

    xf = x.reshape(t, d).astype(F32)
    xb = xf.astype(BF16)
    mem_b = mem.reshape(nb * mem.shape[1], d).astype(BF16)

    for l in range(depth):
        wl = w_in[l]
        w_g = wl[:, :o_beta].astype(BF16)
        w_f = wl[:, o_fox:o_fb].astype(BF16)
        w_h = wl[:, o_hgrn:].astype(BF16)
        w_gate = jnp.zeros((d, LANES), F32)
        w_gate = w_gate.at[:, GATE_BETA:GATE_BETA + gdn_h].set(wl[:, o_beta:o_dec])
        w_gate = w_gate.at[:, GATE_DEC:GATE_DEC + gdn_h].set(wl[:, o_dec:o_fox])
        w_gate = w_gate.at[:, GATE_FOX:GATE_FOX + fox_h].set(wl[:, o_fb:o_hgrn])

        pg = _matmul(xb, w_g, BF16)
        pf = _matmul(xb, w_f, BF16)
        ph = _matmul(xb, w_h, BF16)
        gates = _matmul(xf, w_gate, F32, tn=LANES, hi=True)
        gates_t = gates.T

        o_a = _gdn(pg, gates, gates_t, conv_w[l], gdn_a_log[l], gdn_dt_bias[l], gdn_norm_g[l],
                   nb, gdn_h)
        fbias = jnp.zeros((SUBLANES, 1), F32).at[:fox_h, 0].set(fox_f_bias[l].astype(F32))
        cum = _fox_cum(gates_t, fbias, nb)
        o_b = _fox_attention(pf, cum, nb, fox_h)
        o_c = _hgrn(ph, lower_bounds[l].reshape(1, hw), hgrn_norm_g[l], nb, hgrn_h)

        blocks = [(gw, 0), (fw, gw // fw), (hw, (gw + fw) // hw)]
        xf, xb = _matmul_res_ln([o_a, o_b, o_c], w_out[l].astype(BF16), blocks, xf,
                                ln_g[l, 0], ln_b[l, 0], alpha)

        km = _matmul(mem_b, xa_wk[l].astype(BF16), BF16).reshape(nb, -1, d)
        vm = _matmul(mem_b, xa_wv[l].astype(BF16), BF16).reshape(nb, -1, d)
        xo = _xattn(xb, xa_wq[l].astype(BF16), km, vm, nb)
        xf, xb = _matmul_res_ln([xo], xa_wo[l].astype(BF16), [(d, 0)], xf,
                                ln_g[l, 1], ln_b[l, 1], alpha)

        i = l // 2
        if l % 2 == 0:
            xf, xb = _ffn(xb, ffn_wg[i].astype(BF16), ffn_wu[i].astype(BF16),
                          ffn_wd[i].astype(BF16), xf, ln_g[l, 2], ln_b[l, 2], alpha)
        else:
            xf, xb = _moe(xf, xb, router_w[i], moe_wg[i].astype(BF16), moe_wu[i].astype(BF16),
                          moe_wd[i].astype(BF16), ln_g[l, 2], ln_b[l, 2], alpha)
    return xf.reshape(nb, s, d).astype(x.dtype)
```

```python
import functools

import jax
import jax.numpy as jnp
from jax import lax
from jax.experimental import pallas as pl
from jax.experimental.pallas import tpu as pltpu

F32 = jnp.float32
BF16 = jnp.bfloat16
HI = lax.Precision.HIGHEST

HEAD_DIM = 128
CONV_K = 4
XA_HEADS = 4
N_EXPERTS = 8
TOP_K = 2
LN_EPS = 1e-5
RMS_EPS = 1e-6
L2_EPS = 1e-6
MASK_VALUE = -1e30
LANES = 128
SUBLANES = 8
VMEM_LIMIT = 56 * 1024 * 1024

GATE_BETA, GATE_DEC, GATE_FOX = 0, 8, 16
FFN_TILE = 512
MOE_TILE = 256


def _cparams(*sem):
    return pltpu.CompilerParams(dimension_semantics=sem, vmem_limit_bytes=VMEM_LIMIT)


def _pick(n, pref):
    t = min(pref, n)
    while n % t:
        t //= 2
    return t


def _dot(a, b):
    return jnp.dot(a, b, preferred_element_type=F32)


def _dot_nt(a, b):
    return lax.dot_general(a, b, (((1,), (1,)), ((), ())), preferred_element_type=F32)


def _dot_hi(a, b):
    return jnp.dot(a, b, preferred_element_type=F32, precision=HI)


def _sigmoid(x):
    return 1.0 / (1.0 + jnp.exp(-x))


def _silu(x):
    return x * _sigmoid(x)


def _softplus(x):
    return jnp.maximum(x, 0.0) + jnp.log(1.0 + jnp.exp(-jnp.abs(x)))


def _layer_norm(y, g, b):
    mu = jnp.mean(y, axis=-1, keepdims=True)
    d = y - mu
    var = jnp.mean(d * d, axis=-1, keepdims=True)
    return d * lax.rsqrt(var + LN_EPS) * g + b


def _mm_kernel(x_ref, w_ref, o_ref, *, hi):
    if hi:
        acc = _dot_hi(x_ref[...], w_ref[...])
    else:
        acc = _dot(x_ref[...], w_ref[...])
    o_ref[...] = acc.astype(o_ref.dtype)


def _matmul(x, w, out_dtype, *, tm=1024, tn=512, hi=False):
    m, k = x.shape
    n = w.shape[1]
    tm, tn = _pick(m, tm), _pick(n, tn)
    return pl.pallas_call(
        functools.partial(_mm_kernel, hi=hi),
        grid=(m // tm, n // tn),
        in_specs=[pl.BlockSpec((tm, k), lambda i, j: (i, 0)),
                  pl.BlockSpec((k, tn), lambda i, j: (0, j))],
        out_specs=pl.BlockSpec((tm, tn), lambda i, j: (i, j)),
        out_shape=jax.ShapeDtypeStruct((m, n), out_dtype),
        compiler_params=_cparams("parallel", "arbitrary"),
        name="matmul_hi" if hi else "matmul",
    )(x, w)


def _mm_res_ln_kernel(*refs, alpha, n_in):
    a_refs, w_refs = refs[:n_in], refs[n_in:2 * n_in]
    r_ref, g_ref, b_ref, of_ref, ob_ref = refs[2 * n_in:]
    y = alpha * r_ref[...]
    for a_ref, w_ref in zip(a_refs, w_refs):
        y = y + _dot(a_ref[...], w_ref[...])
    out = _layer_norm(y, g_ref[...], b_ref[...])
    of_ref[...] = out
    ob_ref[...] = out.astype(BF16)


def _matmul_res_ln(a_list, w, w_row_blocks, resid, g, b, alpha, *, tm=512):
    m, d = resid.shape
    tm = _pick(m, tm)
    n_in = len(a_list)
    row = lambda i: (i, 0)
    fixed = lambda i: (0, 0)
    a_specs = [pl.BlockSpec((tm, a.shape[1]), row) for a in a_list]
    w_specs = [pl.BlockSpec((rows, d), functools.partial(lambda i, bi: (bi, 0), bi=bi))
               for rows, bi in w_row_blocks]
    return pl.pallas_call(
        functools.partial(_mm_res_ln_kernel, alpha=alpha, n_in=n_in),
        grid=(m // tm,),
        in_specs=a_specs + w_specs + [pl.BlockSpec((tm, d), row), pl.BlockSpec((1, d), fixed),
                                      pl.BlockSpec((1, d), fixed)],
        out_specs=[pl.BlockSpec((tm, d), row), pl.BlockSpec((tm, d), row)],
        out_shape=[jax.ShapeDtypeStruct((m, d), F32), jax.ShapeDtypeStruct((m, d), BF16)],
        compiler_params=_cparams("parallel"),
        name="matmul_res_ln",
    )(*a_list, *([w] * n_in), resid, g.reshape(1, d), b.reshape(1, d))


def _swiglu_up(x_ref, wg_ref, wu_ref):
    x = x_ref[...]
    return (_silu(_dot(x, wg_ref[...])) * _dot(x, wu_ref[...])).astype(BF16)


def _ffn_kernel(x_ref, wg_ref, wu_ref, wd_ref, r_ref, g_ref, b_ref,
                of_ref, ob_ref, acc_ref, h_ref, *, alpha):
    j = pl.program_id(1)
    nf = pl.num_programs(1) - 1

    @pl.when(j == 0)
    def _():
        acc_ref[...] = jnp.zeros_like(acc_ref)
        h_ref[...] = _swiglu_up(x_ref, wg_ref, wu_ref)

    @pl.when((j > 0) & (j < nf))
    def _():
        h_next = _swiglu_up(x_ref, wg_ref, wu_ref)
        acc_ref[...] += _dot(h_ref[...], wd_ref[...])
        h_ref[...] = h_next

    @pl.when(j == nf)
    def _():
        y = acc_ref[...] + _dot(h_ref[...], wd_ref[...])
        out = _layer_norm(alpha * r_ref[...] + y, g_ref[...], b_ref[...])
        of_ref[...] = out
        ob_ref[...] = out.astype(BF16)


def _ff_tiles(w, tf):
    *lead, d, f = w.shape
    w = w.reshape(*lead, d, f // tf, tf)
    return jnp.swapaxes(w, -3, -2).astype(BF16)


def _ffn(xb, wg, wu, wd, resid, g, b, alpha, *, tm=512):
    m, d = xb.shape
    nf, _, tf = wg.shape
    tm = _pick(m, tm)
    row = lambda i, j: (i, 0)
    fixed = lambda i, j: (0, 0)
    up = lambda i, j: (jnp.minimum(j, nf - 1), 0, 0)
    down = lambda i, j: (jnp.maximum(j - 1, 0), 0)
    return pl.pallas_call(
        functools.partial(_ffn_kernel, alpha=alpha),
        grid=(m // tm, nf + 1),
        in_specs=[pl.BlockSpec((tm, d), row),
                  pl.BlockSpec((None, d, tf), up), pl.BlockSpec((None, d, tf), up),
                  pl.BlockSpec((tf, d), down),
                  pl.BlockSpec((tm, d), row),
                  pl.BlockSpec((1, d), fixed), pl.BlockSpec((1, d), fixed)],
        out_specs=[pl.BlockSpec((tm, d), row), pl.BlockSpec((tm, d), row)],
        out_shape=[jax.ShapeDtypeStruct((m, d), F32), jax.ShapeDtypeStruct((m, d), BF16)],
        scratch_shapes=[pltpu.VMEM((tm, d), F32), pltpu.VMEM((tm, tf), BF16)],
        compiler_params=_cparams("parallel", "arbitrary"),
        name="ffn_dense",
    )(xb, wg, wu, wd, resid, g.reshape(1, d), b.reshape(1, d))


R_E1, R_E2, R_RANK1, R_RANK2, R_G1, R_G2 = range(6)
ROW_SLABS = 16


def _route_kernel(l_ref, tril_ref, info_ref, cnt_ref, carry_ref):
    @pl.when(pl.program_id(0) == 0)
    def _():
        carry_ref[...] = jnp.zeros_like(carry_ref)

    logits = l_ref[...]
    lane = lax.broadcasted_iota(jnp.int32, logits.shape, 1)
    logits = jnp.where(lane < N_EXPERTS, logits, -jnp.inf)
    m1 = jnp.max(logits, axis=-1, keepdims=True)
    i1 = jnp.min(jnp.where(logits == m1, lane, LANES), axis=-1, keepdims=True)
    rest = jnp.where(lane == i1, -jnp.inf, logits)
    m2 = jnp.max(rest, axis=-1, keepdims=True)
    i2 = jnp.min(jnp.where(rest == m2, lane, LANES), axis=-1, keepdims=True)
    e2 = jnp.exp(m2 - m1)
    g1 = 1.0 / (1.0 + e2)
    g2 = e2 / (1.0 + e2)
    onehot = ((lane == i1) | (lane == i2)).astype(F32)
    before = _dot_hi(tril_ref[...], onehot) + carry_ref[...]
    rank1 = jnp.sum(jnp.where(lane == i1, before, 0.0), axis=-1, keepdims=True)
    rank2 = jnp.sum(jnp.where(lane == i2, before, 0.0), axis=-1, keepdims=True)
    info = jnp.zeros(logits.shape, F32)
    for ln, val in ((R_E1, i1.astype(F32)), (R_E2, i2.astype(F32)), (R_RANK1, rank1),
                    (R_RANK2, rank2), (R_G1, g1), (R_G2, g2)):
        info = jnp.where(lane == ln, val, info)
    info_ref[...] = info
    carry_ref[...] += jnp.sum(onehot, axis=0, keepdims=True)
    cnt_ref[...] = carry_ref[...]


def _route(logits, *, tm=512):
    m = logits.shape[0]
    tm = _pick(m, tm)
    r = lax.broadcasted_iota(jnp.int32, (tm, tm), 0)
    c = lax.broadcasted_iota(jnp.int32, (tm, tm), 1)
    strict = (c < r).astype(F32)
    spec = pl.BlockSpec((tm, LANES), lambda i: (i, 0))
    return pl.pallas_call(
        _route_kernel, grid=(m // tm,),
        in_specs=[spec, pl.BlockSpec((tm, tm), lambda i: (0, 0))],
        out_specs=[spec, pl.BlockSpec((1, LANES), lambda i: (0, 0))],
        out_shape=[jax.ShapeDtypeStruct((m, LANES), F32), jax.ShapeDtypeStruct((1, LANES), F32)],
        scratch_shapes=[pltpu.VMEM((1, LANES), F32)],
        compiler_params=_cparams("arbitrary"), name="route_top2",
    )(logits, strict)


def _slot_index(pos_smem, k, r, rows_per_slot):
    return pos_smem[k * rows_per_slot + lax.shift_right_logical(r, 7), r & (LANES - 1)]


def _dispatch_kernel(pos_hbm, x_ref, zero_hbm, xs_hbm, pos_smem, psem, sem, *, tm):
    del zero_hbm
    i = pl.program_id(0)
    pcopy = pltpu.make_async_copy(pos_hbm.at[i], pos_smem, psem)
    pcopy.start()
    pcopy.wait()
    rps = tm // LANES

    def row_copy(r, k):
        return pltpu.make_async_copy(x_ref.at[r], xs_hbm.at[_slot_index(pos_smem, k, r, rps)], sem)

    def issue(r, c):
        for k in range(TOP_K):
            row_copy(r, k).start()
        return c

    def drain(r, c):
        for k in range(TOP_K):
            row_copy(r, k).wait()
        return c

    lax.fori_loop(0, tm, issue, 0)
    lax.fori_loop(0, tm, drain, 0)


def _moe_dispatch(x3, pos3, n_rows, *, tm):
    t = x3.shape[0]
    zeros = jnp.zeros((n_rows,) + x3.shape[1:], x3.dtype)
    any_spec = pl.BlockSpec(memory_space=pl.ANY)
    return pl.pallas_call(
        functools.partial(_dispatch_kernel, tm=tm),
        grid=(t // tm,),
        in_specs=[any_spec, pl.BlockSpec((tm,) + x3.shape[1:], lambda i: (i, 0, 0)), any_spec],
        out_specs=any_spec,
        out_shape=jax.ShapeDtypeStruct(zeros.shape, zeros.dtype),
        scratch_shapes=[pltpu.SMEM(pos3.shape[1:], jnp.int32),
                        pltpu.SemaphoreType.DMA, pltpu.SemaphoreType.DMA],
        input_output_aliases={2: 0},
        compiler_params=_cparams("arbitrary"),
        name="moe_dispatch",
    )(pos3, x3, zeros)


def _moe_ffn_kernel(te_ref, nv_ref, x_ref, wg_ref, wu_ref, wd_ref, o_ref, acc_ref, h_ref):
    i, j = pl.program_id(0), pl.program_id(1)
    nf = pl.num_programs(1) - 1
    valid = i < nv_ref[0]

    @pl.when(valid & (j == 0))
    def _():
        acc_ref[...] = jnp.zeros_like(acc_ref)
        h_ref[...] = _swiglu_up(x_ref, wg_ref, wu_ref)

    @pl.when(valid & (j > 0) & (j < nf))
    def _():
        h_next = _swiglu_up(x_ref, wg_ref, wu_ref)
        acc_ref[...] += _dot(h_ref[...], wd_ref[...])
        h_ref[...] = h_next

    @pl.when(valid & (j == nf))
    def _():
        o_ref[...] = acc_ref[...] + _dot(h_ref[...], wd_ref[...])

    @pl.when(jnp.logical_not(valid) & (j == nf))
    def _():
        o_ref[...] = jnp.zeros_like(o_ref)


def _moe_ffn(xs, wg, wu, wd, tile_expert, n_valid, *, tm):
    n_rows, d = xs.shape
    _, nf, _, tf = wg.shape

    def up(i, j, nv):
        return jnp.where(i < nv[0], jnp.minimum(j, nf - 1), nf - 1)

    def down(i, j, nv):
        return jnp.where(i < nv[0], jnp.maximum(j - 1, 0), nf - 1)

    grid_spec = pltpu.PrefetchScalarGridSpec(
        num_scalar_prefetch=2,
        grid=(n_rows // tm, nf + 1),
        in_specs=[pl.BlockSpec((tm, d), lambda i, j, te, nv: (i, 0)),
                  pl.BlockSpec((None, None, d, tf), lambda i, j, te, nv: (te[i], up(i, j, nv), 0, 0)),
                  pl.BlockSpec((None, None, d, tf), lambda i, j, te, nv: (te[i], up(i, j, nv), 0, 0)),
                  pl.BlockSpec((None, tf, d), lambda i, j, te, nv: (te[i], down(i, j, nv), 0))],
        out_specs=pl.BlockSpec((tm, d), lambda i, j, te, nv: (i, 0)),
        scratch_shapes=[pltpu.VMEM((tm, d), F32), pltpu.VMEM((tm, tf), BF16)])
    return pl.pallas_call(
        _moe_ffn_kernel, grid_spec=grid_spec,
        out_shape=jax.ShapeDtypeStruct((n_rows, d), F32),
        compiler_params=_cparams("arbitrary", "arbitrary"),
        name="moe_ffn",
    )(tile_expert, n_valid, xs, wg, wu, wd)


def _combine_kernel(pos_hbm, ys_hbm, info_ref, r_ref, g_ref, b_ref, of_ref, ob_ref,
                    pos_smem, buf_ref, z_ref, psem, sem, *, tm, alpha):
    i = pl.program_id(0)
    pcopy = pltpu.make_async_copy(pos_hbm.at[i], pos_smem, psem)
    pcopy.start()
    pcopy.wait()
    rps = tm // LANES

    def row_copy(r, k):
        dst = buf_ref.at[k, pl.ds(pl.multiple_of(r * ROW_SLABS, ROW_SLABS), ROW_SLABS)]
        return pltpu.make_async_copy(ys_hbm.at[_slot_index(pos_smem, k, r, rps)], dst, sem)

    def issue(r, c):
        for k in range(TOP_K):
            row_copy(r, k).start()
        return c

    def drain(r, c):
        for k in range(TOP_K):
            row_copy(r, k).wait()
        return c

    lax.fori_loop(0, tm, issue, 0)
    lax.fori_loop(0, tm, drain, 0)

    info = info_ref[...]
    g1 = info[:, R_G1:R_G1 + 1]
    g2 = info[:, R_G2:R_G2 + 1]
    for a in range(ROW_SLABS):
        y1 = buf_ref[0, pl.ds(a, tm, stride=ROW_SLABS), :]
        y2 = buf_ref[1, pl.ds(a, tm, stride=ROW_SLABS), :]
        z_ref[:, a * LANES:(a + 1) * LANES] = g1 * y1 + g2 * y2
    out = _layer_norm(alpha * r_ref[...] + z_ref[...], g_ref[...], b_ref[...])
    of_ref[...] = out
    ob_ref[...] = out.astype(BF16)


def _moe_combine(ys3, pos3, info, resid, g, b, alpha, *, tm):
    t, d = resid.shape
    row = lambda i: (i, 0)
    fixed = lambda i: (0, 0)
    any_spec = pl.BlockSpec(memory_space=pl.ANY)
    return pl.pallas_call(
        functools.partial(_combine_kernel, tm=tm, alpha=alpha),
        grid=(t // tm,),
        in_specs=[any_spec, any_spec, pl.BlockSpec((tm, LANES), row), pl.BlockSpec((tm, d), row),
                  pl.BlockSpec((1, d), fixed), pl.BlockSpec((1, d), fixed)],
        out_specs=[pl.BlockSpec((tm, d), row), pl.BlockSpec((tm, d), row)],
        out_shape=[jax.ShapeDtypeStruct((t, d), F32), jax.ShapeDtypeStruct((t, d), BF16)],
        scratch_shapes=[pltpu.SMEM(pos3.shape[1:], jnp.int32),
                        pltpu.VMEM((TOP_K, tm * ROW_SLABS, LANES), F32),
                        pltpu.VMEM((tm, d), F32),
                        pltpu.SemaphoreType.DMA, pltpu.SemaphoreType.DMA],
        compiler_params=_cparams("arbitrary"),
        name="moe_combine",
    )(pos3, ys3, info, resid, g.reshape(1, d), b.reshape(1, d))


def _slot_table(pos, tm):
    t = pos.shape[1]
    p = pos.reshape(TOP_K, t // tm, tm // LANES, LANES)
    return jnp.transpose(p, (1, 0, 2, 3)).reshape(t // tm, TOP_K * tm // LANES, LANES)


def _moe(xf, xb, router_w, wg, wu, wd, g, b, alpha, *, tm_rows=1024, tm_disp=512, tm_comb=256):
    t, d = xf.shape
    ne = router_w.shape[1]
    w_r = jnp.zeros((d, LANES), F32).at[:, :ne].set(router_w.astype(F32))
    logits = _matmul(xf, w_r, F32, tn=LANES, hi=True)
    info, counts = _route(logits)

    tm_rows = _pick(TOP_K * t, tm_rows)
    n_tiles = TOP_K * t // tm_rows + ne
    counts = counts[0, :ne].astype(jnp.int32)
    tiles_e = (counts + tm_rows - 1) // tm_rows
    tile_end = jnp.cumsum(tiles_e)
    row_off = (tile_end - tiles_e) * tm_rows
    tile_ids = jnp.arange(n_tiles, dtype=jnp.int32)
    tile_expert = jnp.sum(tile_ids[:, None] >= tile_end[None, :], axis=1).astype(jnp.int32)
    tile_expert = jnp.minimum(tile_expert, ne - 1)
    last_e = jnp.max(jnp.where(tiles_e > 0, jnp.arange(ne, dtype=jnp.int32), 0))
    n_valid = tile_end[-1:].astype(jnp.int32)
    tile_expert = jnp.where(tile_ids < n_valid[0], tile_expert, last_e)
    e_idx = info[:, R_E1:R_E2 + 1].astype(jnp.int32)
    rank = info[:, R_RANK1:R_RANK2 + 1].astype(jnp.int32)
    off = jnp.sum(jnp.where(e_idx[:, :, None] == jnp.arange(ne, dtype=jnp.int32),
                            row_off[None, None, :], 0), axis=-1)
    pos = (off + rank).T

    tm_disp, tm_comb = _pick(t, tm_disp), _pick(t, tm_comb)
    n_rows = n_tiles * tm_rows
    x3 = xb.reshape(t, ROW_SLABS, d // ROW_SLABS)
    xs = _moe_dispatch(x3, _slot_table(pos, tm_disp), n_rows, tm=tm_disp)
    ys = _moe_ffn(xs.reshape(n_rows, d), wg, wu, wd, tile_expert, n_valid, tm=tm_rows)
    ys3 = ys.reshape(n_rows, ROW_SLABS, d // ROW_SLABS)
    return _moe_combine(ys3, _slot_table(pos, tm_comb), info, xf, g, b, alpha, tm=tm_comb)


def _xattn_kernel(x_ref, wq_ref, k_ref, v_ref, o_ref):
    d = wq_ref.shape[1]
    hd = d // XA_HEADS
    q = _dot(x_ref[...], wq_ref[...]).astype(BF16)
    for h in range(XA_HEADS):
        sl = slice(h * hd, (h + 1) * hd)
        s = _dot_nt(q[:, sl], k_ref[:, sl]) * (hd ** -0.5)
        s = s - jnp.max(s, axis=-1, keepdims=True)
        p = jnp.exp(s)
        p = p / jnp.sum(p, axis=-1, keepdims=True)
        o_ref[:, sl] = _dot(p.astype(BF16), v_ref[:, sl]).astype(o_ref.dtype)


def _xattn(xb, wq, km, vm, nb, *, tm=256):
    t, d = xb.shape
    s = t // nb
    ml = km.shape[1]
    tm = _pick(s, tm)
    ns = s // tm
    return pl.pallas_call(
        _xattn_kernel,
        grid=(nb, ns),
        in_specs=[pl.BlockSpec((tm, d), lambda b, i: (b * ns + i, 0)),
                  pl.BlockSpec((d, d), lambda b, i: (0, 0)),
                  pl.BlockSpec((None, ml, d), lambda b, i: (b, 0, 0)),
                  pl.BlockSpec((None, ml, d), lambda b, i: (b, 0, 0))],
        out_specs=pl.BlockSpec((tm, d), lambda b, i: (b * ns + i, 0)),
        out_shape=jax.ShapeDtypeStruct((t, d), BF16),
        compiler_params=_cparams("parallel", "arbitrary"),
        name="xattn",
    )(xb, wq, km, vm)


def _fox_cum_kernel(gt_ref, bias_ref, c_ref, carry_ref):
    @pl.when(pl.program_id(1) == 0)
    def _():
        carry_ref[...] = jnp.zeros_like(carry_ref)

    z = gt_ref[...] + bias_ref[...]
    lf = -_softplus(-z)
    tk = z.shape[1]
    r = lax.broadcasted_iota(jnp.int32, (tk, tk), 0)
    c = lax.broadcasted_iota(jnp.int32, (tk, tk), 1)
    upper = (r <= c).astype(F32)
    cum = _dot_hi(lf, upper) + carry_ref[...]
    c_ref[...] = cum
    carry_ref[...] = cum[:, tk - 1:tk]


def _fox_cum(gates_t, bias, nb, *, tk=512):
    t = gates_t.shape[1]
    s = t // nb
    tk = _pick(s, tk)
    ns = s // tk
    rb = GATE_FOX // SUBLANES
    return pl.pallas_call(
        _fox_cum_kernel,
        grid=(nb, ns),
        in_specs=[pl.BlockSpec((SUBLANES, tk), lambda b, i: (rb, b * ns + i)),
                  pl.BlockSpec((SUBLANES, 1), lambda b, i: (0, 0))],
        out_specs=pl.BlockSpec((None, SUBLANES, tk), lambda b, i: (b, 0, i)),
        out_shape=jax.ShapeDtypeStruct((nb, SUBLANES, s), F32),
        scratch_shapes=[pltpu.VMEM((SUBLANES, 1), F32)],
        compiler_params=_cparams("parallel", "arbitrary"),
        name="fox_cumsum",
    )(gates_t, bias)


def _fox_kernel(q_ref, k_ref, v_ref, c_ref, o_ref, *, tq, tk, scale):
    assert tq == tk
    qi = pl.program_id(2)
    q = q_ref[...]
    q0 = pl.multiple_of(qi * tq, tq)
    c_q0 = c_ref[:, pl.ds(q0, LANES)][:, 0:1]

    def step(kj, carry, diagonal):
        m, l, acc = carry
        k0 = pl.multiple_of(kj * tk, tk)
        kb = k_ref[pl.ds(k0, tk), :]
        vb = v_ref[pl.ds(k0, tk), :]
        s = _dot_nt(q, kb) * scale + (c_q0 - c_ref[:, pl.ds(k0, tk)])
        if diagonal:
            row = lax.broadcasted_iota(jnp.int32, (tq, tk), 0)
            col = lax.broadcasted_iota(jnp.int32, (tq, tk), 1)
            s = jnp.where(col <= row, s, MASK_VALUE)
        m_new = jnp.maximum(m, jnp.max(s, axis=-1, keepdims=True))
        a = jnp.exp(m - m_new)
        p = jnp.exp(s - m_new)
        l = a * l + jnp.sum(p, axis=-1, keepdims=True)
        acc = a * acc + _dot(p.astype(BF16), vb)
        return m_new, l, acc

    init = (jnp.full((tq, 1), MASK_VALUE, F32), jnp.zeros((tq, 1), F32),
            jnp.zeros((tq, HEAD_DIM), F32))
    carry = lax.fori_loop(0, qi, functools.partial(step, diagonal=False), init)
    m, l, acc = step(qi, carry, True)
    o_ref[...] = (acc / l).astype(o_ref.dtype)


def _fox_attention(proj, cum, nb, nh, *, tq=512, tk=512):
    t = proj.shape[0]
    s = t // nb
    tq, tk = _pick(s, tq), _pick(s, tk)
    nq = s // tq
    proj3 = proj.reshape(nb, s, proj.shape[1])
    cum2 = cum.reshape(nb * SUBLANES, 1, s)
    out = pl.pallas_call(
        functools.partial(_fox_kernel, tq=tq, tk=tk, scale=HEAD_DIM ** -0.5),
        grid=(nb, nh, nq),
        in_specs=[pl.BlockSpec((None, tq, HEAD_DIM), lambda b, h, i: (b, i, h)),
                  pl.BlockSpec((None, s, HEAD_DIM), lambda b, h, i: (b, 0, nh + h)),
                  pl.BlockSpec((None, s, HEAD_DIM), lambda b, h, i: (b, 0, 2 * nh + h)),
                  pl.BlockSpec((None, 1, s), lambda b, h, i: (b * SUBLANES + h, 0, 0))],
        out_specs=pl.BlockSpec((None, tq, HEAD_DIM), lambda b, h, i: (b, i, h)),
        out_shape=jax.ShapeDtypeStruct((nb, s, nh * HEAD_DIM), BF16),
        compiler_params=_cparams("parallel", "parallel", "arbitrary"),
        name="fox_attention",
    )(proj3, proj3, proj3, cum2)
    return out.reshape(t, nh * HEAD_DIM)


HGRN_SUB = 16


def _hgrn_kernel(p_ref, lb_ref, ng_ref, tri_ref, o_ref, st_ref, *, nh, c):
    @pl.when(pl.program_id(1) == 0)
    def _():
        st_ref[...] = jnp.zeros_like(st_ref)

    w = nh * HEAD_DIM
    row = lax.broadcasted_iota(jnp.int32, (c, c), 0)
    col = lax.broadcasted_iota(jnp.int32, (c, c), 1)
    tri = tri_ref[...]
    heads = range(nh)
    sl = lambda grp, h: slice(grp * w + h * HEAD_DIM, grp * w + (h + 1) * HEAD_DIM)
    q = [_silu(p_ref[:, sl(0, h)].astype(F32)) for h in heads]
    f = [lb_ref[:, sl(0, h)] + (1.0 - lb_ref[:, sl(0, h)]) * _sigmoid(p_ref[:, sl(1, h)].astype(F32))
         for h in heads]
    k = [1.0 - f[h] for h in heads]
    vb = [p_ref[:, sl(2, h)] for h in heads]
    bc = [_dot_hi(tri, jnp.log(f[h])) for h in heads]

    def scaled(x, b3, shift):
        return (x.reshape(b3.shape) * jnp.exp(shift)).reshape(c, HEAD_DIM).astype(BF16)

    nsub = c // HGRN_SUB
    same = (row // HGRN_SUB == col // HGRN_SUB) & (col <= row)
    a = []
    for h in heads:
        b3 = bc[h].reshape(nsub, HGRN_SUB, HEAD_DIM)
        ref = b3[:, HGRN_SUB // 2 - 1:HGRN_SUB // 2, :]
        a.append(jnp.where(same, _dot_nt(scaled(q[h], b3, b3 - ref), scaled(k[h], b3, ref - b3)), 0.0))
    s = HGRN_SUB
    while s < c:
        rb, cb = row // s, col // s
        pair = (rb % 2 == 1) & (cb == rb - 1)
        for h in heads:
            b3 = bc[h].reshape(c // s, s, HEAD_DIM)
            last = b3[:, s - 1:s, :]
            prev = jnp.concatenate([jnp.zeros_like(last[:1]), last[:-1]], axis=0)
            a[h] = a[h] + jnp.where(
                pair, _dot_nt(scaled(q[h], b3, b3 - prev), scaled(k[h], b3, last - b3)), 0.0)
        s *= 2

    st = [st_ref[h] for h in heads]
    o = [_dot(a[h].astype(BF16), vb[h])
         + _dot_nt((q[h] * jnp.exp(bc[h])).astype(BF16), st[h].astype(BF16)) for h in heads]
    for h in heads:
        b_last = bc[h][c - 1:c, :]
        k_st = k[h] * jnp.exp(b_last - bc[h])
        st_ref[h] = (st[h] * jnp.exp(b_last)
                     + _dot(vb[h].astype(F32).T.astype(BF16), k_st.astype(BF16)))
    for h in heads:
        gate = _sigmoid(p_ref[:, sl(3, h)].astype(F32))
        oh = o[h] * lax.rsqrt(jnp.mean(o[h] * o[h], axis=-1, keepdims=True) + RMS_EPS) * ng_ref[...]
        o_ref[:, h * HEAD_DIM:(h + 1) * HEAD_DIM] = (oh * gate).astype(o_ref.dtype)


def _tri_lower(c):
    r = lax.broadcasted_iota(jnp.int32, (c, c), 0)
    col = lax.broadcasted_iota(jnp.int32, (c, c), 1)
    return (col <= r).astype(F32)


def _hgrn(proj, lb, norm_g, nb, nh, *, c=128):
    t, wtot = proj.shape
    s = t // nb
    c = _pick(s, c)
    ns = s // c
    w = nh * HEAD_DIM
    fixed = lambda b, i: (0, 0)
    return pl.pallas_call(
        functools.partial(_hgrn_kernel, nh=nh, c=c),
        grid=(nb, ns),
        in_specs=[pl.BlockSpec((c, wtot), lambda b, i: (b * ns + i, 0)),
                  pl.BlockSpec((1, w), fixed), pl.BlockSpec((1, HEAD_DIM), fixed),
                  pl.BlockSpec((c, c), fixed)],
        out_specs=pl.BlockSpec((c, w), lambda b, i: (b * ns + i, 0)),
        out_shape=jax.ShapeDtypeStruct((t, w), BF16),
        scratch_shapes=[pltpu.VMEM((nh, HEAD_DIM, HEAD_DIM), F32)],
        compiler_params=_cparams("parallel", "arbitrary"),
        name="hgrn2",
    )(proj, lb, norm_g.reshape(1, HEAD_DIM), _tri_lower(c))


def _level_masks(c):
    r = lax.broadcasted_iota(jnp.int32, (c, c), 0)
    col = lax.broadcasted_iota(jnp.int32, (c, c), 1)
    out, s = [], 1
    while s < c:
        rb, cb = r // s, col // s
        out.append(((rb % 2 == 1) & (cb == rb - 1)).astype(F32))
        s *= 2
    return jnp.stack(out)


def _gdn_kernel(p_ref, halo_ref, gt_ref, gtt_ref, cw_ref, av_ref, dtv_ref, avt_ref, dtt_ref,
                ng_ref, tri_ref, lvl_ref, o_ref, s_ref, *, nh, c):
    first = pl.program_id(1) == 0

    @pl.when(first)
    def _():
        s_ref[...] = jnp.zeros_like(s_ref)

    w = nh * HEAD_DIM
    row = lax.broadcasted_iota(jnp.int32, (c, c), 0)
    col = lax.broadcasted_iota(jnp.int32, (c, c), 1)
    causal = col <= row
    tri = tri_ref[...]
    gates = gt_ref[...]
    la = -av_ref[...] * _softplus(gates + dtv_ref[...])
    g_col = _dot_hi(tri, la)
    la_t = -avt_ref[...] * _softplus(gtt_ref[...] + dtt_ref[...])
    g_row = lax.dot_general(la_t, tri, (((1,), (1,)), ((), ())),
                            preferred_element_type=F32, precision=HI)
    halo_scale = jnp.where(first, 0.0, 1.0)
    sub = lax.broadcasted_iota(jnp.int32, (SUBLANES, HEAD_DIM), 0)

    def conv_silu(grp, h):
        lo = grp * w + h * HEAD_DIM
        cur = p_ref[:, lo:lo + HEAD_DIM].astype(F32)
        halo = halo_ref[:, lo:lo + HEAD_DIM].astype(F32) * halo_scale
        cw = cw_ref[:, lo:lo + HEAD_DIM]
        y = cw[CONV_K - 1:CONV_K, :] * cur
        for j in range(1, CONV_K):
            rolled = pltpu.roll(cur, j, axis=0)
            head = jnp.where(sub < j, pltpu.roll(halo, j, axis=0), rolled[:SUBLANES])
            shifted = jnp.concatenate([head, rolled[SUBLANES:]], axis=0)
            y = y + cw[CONV_K - 1 - j:CONV_K - j, :] * shifted
        return _silu(y)

    def l2n(x):
        return x * lax.rsqrt(jnp.sum(x * x, axis=-1, keepdims=True) + L2_EPS)

    heads = range(nh)
    q = [l2n(conv_silu(0, h)) * (HEAD_DIM ** -0.5) for h in heads]
    k = [l2n(conv_silu(1, h)) for h in heads]
    v = [conv_silu(2, h) for h in heads]
    beta = [_sigmoid(gates[:, GATE_BETA + h:GATE_BETA + h + 1]) for h in heads]
    gc = [g_col[:, GATE_DEC + h:GATE_DEC + h + 1] for h in heads]
    gamma = [jnp.where(causal, jnp.exp(jnp.where(causal, gc[h] - g_row[h:h + 1, :], 0.0)), 0.0)
             for h in heads]
    kb = [k[h] * beta[h] for h in heads]
    kbf = [k[h].astype(BF16) for h in heads]
    m = [_dot_nt(kb[h].astype(BF16), kbf[h]) * gamma[h] for h in heads]

    n = [-(m[h] * lvl_ref[0]) for h in heads]
    for lv in range(1, lvl_ref.shape[0]):
        low = [m[h] * lvl_ref[lv] for h in heads]
        p = [low[h] + _dot(n[h].astype(BF16), low[h].astype(BF16)) for h in heads]
        n = [n[h] - p[h] - _dot(p[h].astype(BF16), n[h].astype(BF16)) for h in heads]

    eg = [jnp.exp(gc[h]) for h in heads]
    g_last = [gc[h][c - 1:c, :] for h in heads]
    rhs = [jnp.concatenate([v[h] * beta[h], kb[h] * eg[h]], axis=1) for h in heads]
    sol = [rhs[h] + _dot(n[h].astype(BF16), rhs[h].astype(BF16)) for h in heads]
    state = [s_ref[h] for h in heads]
    sb = [state[h].astype(BF16) for h in heads]
    v_new = [sol[h][:, :HEAD_DIM] - _dot(sol[h][:, HEAD_DIM:].astype(BF16), sb[h]) for h in heads]
    qk = [_dot_nt(q[h].astype(BF16), kbf[h]) * gamma[h] for h in heads]
    o = [_dot((q[h] * eg[h]).astype(BF16), sb[h]) + _dot(qk[h].astype(BF16), v_new[h].astype(BF16))
         for h in heads]
    k_dec = [k[h] * jnp.exp(g_last[h] - gc[h]) for h in heads]
    for h in heads:
        s_ref[h] = (state[h] * jnp.exp(g_last[h])
                    + _dot(k_dec[h].T.astype(BF16), v_new[h].astype(BF16)))
    for h in heads:
        z = p_ref[:, 3 * w + h * HEAD_DIM:3 * w + (h + 1) * HEAD_DIM].astype(F32)
        oh = o[h] * lax.rsqrt(jnp.mean(o[h] * o[h], axis=-1, keepdims=True) + RMS_EPS) * ng_ref[...]
        o_ref[:, h * HEAD_DIM:(h + 1) * HEAD_DIM] = (oh * _silu(z)).astype(o_ref.dtype)


def _gdn(proj, gates, gates_t, conv_w, a_log, dt_bias, norm_g, nb, nh, *, c=128):
    t, wtot = proj.shape
    s = t // nb
    c = _pick(s, c)
    ns = s // c
    w = nh * HEAD_DIM
    a = jnp.exp(a_log.astype(F32))
    av = jnp.zeros((1, LANES), F32).at[0, GATE_DEC:GATE_DEC + nh].set(a)
    dtv = jnp.zeros((1, LANES), F32).at[0, GATE_DEC:GATE_DEC + nh].set(dt_bias.astype(F32))
    avt = jnp.zeros((SUBLANES, 1), F32).at[:nh, 0].set(a)
    dtt = jnp.zeros((SUBLANES, 1), F32).at[:nh, 0].set(dt_bias.astype(F32))
    lvl = _level_masks(c)
    fixed = lambda b, i: (0, 0)
    rows_per_halo = c // SUBLANES
    return pl.pallas_call(
        functools.partial(_gdn_kernel, nh=nh, c=c),
        grid=(nb, ns),
        in_specs=[pl.BlockSpec((c, wtot), lambda b, i: (b * ns + i, 0)),
                  pl.BlockSpec((SUBLANES, 3 * w),
                               lambda b, i: (jnp.maximum((b * ns + i) * rows_per_halo - 1, 0), 0)),
                  pl.BlockSpec((c, LANES), lambda b, i: (b * ns + i, 0)),
                  pl.BlockSpec((SUBLANES, c), lambda b, i: (GATE_DEC // SUBLANES, b * ns + i)),
                  pl.BlockSpec((CONV_K, 3 * w), fixed),
                  pl.BlockSpec((1, LANES), fixed), pl.BlockSpec((1, LANES), fixed),
                  pl.BlockSpec((SUBLANES, 1), fixed), pl.BlockSpec((SUBLANES, 1), fixed),
                  pl.BlockSpec((1, HEAD_DIM), fixed),
                  pl.BlockSpec((c, c), fixed),
                  pl.BlockSpec(lvl.shape, lambda b, i: (0, 0, 0))],
        out_specs=pl.BlockSpec((c, w), lambda b, i: (b * ns + i, 0)),
        out_shape=jax.ShapeDtypeStruct((t, w), BF16),
        scratch_shapes=[pltpu.VMEM((nh, HEAD_DIM, HEAD_DIM), F32)],
        compiler_params=_cparams("parallel", "arbitrary"),
        name="gated_delta",
    )(proj, proj, gates, gates_t, conv_w.astype(F32), av, dtv, avt, dtt,
      norm_g.reshape(1, HEAD_DIM), _tri_lower(c), lvl)


def kernel(x, mem, w_in, conv_w, gdn_a_log, gdn_dt_bias, gdn_norm_g, fox_f_bias,
           hgrn_lb_logits, hgrn_norm_g, w_out, xa_wq, xa_wk, xa_wv, xa_wo, ln_g, ln_b,
           ffn_wg, ffn_wu, ffn_wd, router_w, moe_wg, moe_wu, moe_wd):
    nb, s, d = x.shape
    depth = w_in.shape[0]
    t = nb * s
    n_mix = d // HEAD_DIM
    hgrn_h = n_mix // 4
    gdn_h = (n_mix - hgrn_h) // 2
    fox_h = n_mix - hgrn_h - gdn_h
    gw, fw, hw = gdn_h * HEAD_DIM, fox_h * HEAD_DIM, hgrn_h * HEAD_DIM
    alpha = (2.0 * depth) ** 0.25
    ne = router_w.shape[-1]

    o_beta = 4 * gw
    o_dec = o_beta + gdn_h
    o_fox = o_dec + gdn_h
    o_fb = o_fox + 3 * fw
    o_hgrn = o_fb + fox_h

    lb_w = jax.nn.softmax(hgrn_lb_logits.astype(F32), axis=0)
    lower_bounds = jnp.cumsum(lb_w, axis=0) - lb_w[:1]

    xf = x.reshape(t, d).astype(F32)
    xb = xf.astype(BF16)
    mem_b = mem.reshape(nb * mem.shape[1], d).astype(BF16)

    for l in range(depth):
        wl = w_in[l]
        w_g = wl[:, :o_beta].astype(BF16)
        w_f = wl[:, o_fox:o_fb].astype(BF16)
        w_h = wl[:, o_hgrn:].astype(BF16)
        w_gate = jnp.zeros((d, LANES), F32)
        w_gate = w_gate.at[:, GATE_BETA:GATE_BETA + gdn_h].set(wl[:, o_beta:o_dec])
        w_gate = w_gate.at[:, GATE_DEC:GATE_DEC + gdn_h].set(wl[:, o_dec:o_fox])
        w_gate = w_gate.at[:, GATE_FOX:GATE_FOX + fox_h].set(wl[:, o_fb:o_hgrn])

        pg = _matmul(xb, w_g, BF16)
        pf = _matmul(xb, w_f, BF16)
        ph = _matmul(xb, w_h, BF16)
        gates = _matmul(xf, w_gate, F32, tn=LANES, hi=True)
        gates_t = gates.T

        o_a = _gdn(pg, gates, gates_t, conv_w[l], gdn_a_log[l], gdn_dt_bias[l], gdn_norm_g[l],
                   nb, gdn_h)
        fbias = jnp.zeros((SUBLANES, 1), F32).at[:fox_h, 0].set(fox_f_bias[l].astype(F32))
        cum = _fox_cum(gates_t, fbias, nb)
        o_b = _fox_attention(pf, cum, nb, fox_h)
        o_c = _hgrn(ph, lower_bounds[l].reshape(1, hw), hgrn_norm_g[l], nb, hgrn_h)

        blocks = [(gw, 0), (fw, gw // fw), (hw, (gw + fw) // hw)]
        xf, xb = _matmul_res_ln([o_a, o_b, o_c], w_out[l].astype(BF16), blocks, xf,
                                ln_g[l, 0], ln_b[l, 0], alpha)

        km = _matmul(mem_b, xa_wk[l].astype(BF16), BF16).reshape(nb, -1, d)
        vm = _matmul(mem_b, xa_wv[l].astype(BF16), BF16).reshape(nb, -1, d)
        xo = _xattn(xb, xa_wq[l].astype(BF16), km, vm, nb)
        xf, xb = _matmul_res_ln([xo], xa_wo[l].astype(BF16), [(d, 0)], xf,
                                ln_g[l, 1], ln_b[l, 1], alpha)

        i = l // 2
        if l % 2 == 0:
            tf = _pick(ffn_wg.shape[-1], FFN_TILE)
            xf, xb = _ffn(xb, _ff_tiles(ffn_wg[i], tf), _ff_tiles(ffn_wu[i], tf),
                          ffn_wd[i].astype(BF16), xf, ln_g[l, 2], ln_b[l, 2], alpha)
        else:
            tf = _pick(moe_wg.shape[-1], MOE_TILE)
            xf, xb = _moe(xf, xb, router_w[i], _ff_tiles(moe_wg[i], tf), _ff_tiles(moe_wu[i], tf),
                          moe_wd[i].astype(BF16), ln_g[l, 2], ln_b[l, 2], alpha)
    return xf.reshape(nb, s, d).astype(x.dtype)
```

```python
import functools

import jax
import jax.numpy as jnp
from jax import lax
from jax.experimental import pallas as pl
from jax.experimental.pallas import tpu as pltpu

F32 = jnp.float32
BF16 = jnp.bfloat16
HI = lax.Precision.HIGHEST

HEAD_DIM = 128
CONV_K = 4
XA_HEADS = 4
N_EXPERTS = 8
TOP_K = 2
LN_EPS = 1e-5
RMS_EPS = 1e-6
L2_EPS = 1e-6
MASK_VALUE = -1e30
LANES = 128
SUBLANES = 8
VMEM_LIMIT = 56 * 1024 * 1024

GATE_BETA, GATE_DEC, GATE_FOX = 0, 8, 16
FFN_TILE = 512
MOE_TILE = 256


def _cparams(*sem):
    return pltpu.CompilerParams(dimension_semantics=sem, vmem_limit_bytes=VMEM_LIMIT)


def _pick(n, pref):
    t = min(pref, n)
    while n % t:
        t //= 2
    return t


def _dot(a, b):
    return jnp.dot(a, b, preferred_element_type=F32)


def _dot_nt(a, b):
    return lax.dot_general(a, b, (((1,), (1,)), ((), ())), preferred_element_type=F32)


def _dot_hi(a, b):
    return jnp.dot(a, b, preferred_element_type=F32, precision=HI)


def _sigmoid(x):
    return 1.0 / (1.0 + jnp.exp(-x))


def _silu(x):
    return x * _sigmoid(x)


def _softplus(x):
    return jnp.maximum(x, 0.0) + jnp.log(1.0 + jnp.exp(-jnp.abs(x)))


def _layer_norm(y, g, b):
    mu = jnp.mean(y, axis=-1, keepdims=True)
    d = y - mu
    var = jnp.mean(d * d, axis=-1, keepdims=True)
    return d * lax.rsqrt(var + LN_EPS) * g + b


def _mm_kernel(x_ref, w_ref, o_ref, *, hi):
    if hi:
        acc = _dot_hi(x_ref[...], w_ref[...])
    else:
        acc = _dot(x_ref[...], w_ref[...])
    o_ref[...] = acc.astype(o_ref.dtype)


def _matmul(x, w, out_dtype, *, tm=1024, tn=512, hi=False):
    m, k = x.shape
    n = w.shape[1]
    tm, tn = _pick(m, tm), _pick(n, tn)
    return pl.pallas_call(
        functools.partial(_mm_kernel, hi=hi),
        grid=(m // tm, n // tn),
        in_specs=[pl.BlockSpec((tm, k), lambda i, j: (i, 0)),
                  pl.BlockSpec((k, tn), lambda i, j: (0, j))],
        out_specs=pl.BlockSpec((tm, tn), lambda i, j: (i, j)),
        out_shape=jax.ShapeDtypeStruct((m, n), out_dtype),
        compiler_params=_cparams("parallel", "arbitrary"),
        name="matmul_hi" if hi else "matmul",
    )(x, w)


def _mm_res_ln_kernel(*refs, alpha, n_in):
    a_refs, w_refs = refs[:n_in], refs[n_in:2 * n_in]
    r_ref, g_ref, b_ref, of_ref, ob_ref = refs[2 * n_in:]
    y = alpha * r_ref[...]
    for a_ref, w_ref in zip(a_refs, w_refs):
        y = y + _dot(a_ref[...], w_ref[...])
    out = _layer_norm(y, g_ref[...], b_ref[...])
    of_ref[...] = out
    ob_ref[...] = out.astype(BF16)


def _matmul_res_ln(a_list, w, w_row_blocks, resid, g, b, alpha, *, tm=512):
    m, d = resid.shape
    tm = _pick(m, tm)
    n_in = len(a_list)
    row = lambda i: (i, 0)
    fixed = lambda i: (0, 0)
    a_specs = [pl.BlockSpec((tm, a.shape[1]), row) for a in a_list]
    w_specs = [pl.BlockSpec((rows, d), functools.partial(lambda i, bi: (bi, 0), bi=bi))
               for rows, bi in w_row_blocks]
    return pl.pallas_call(
        functools.partial(_mm_res_ln_kernel, alpha=alpha, n_in=n_in),
        grid=(m // tm,),
        in_specs=a_specs + w_specs + [pl.BlockSpec((tm, d), row), pl.BlockSpec((1, d), fixed),
                                      pl.BlockSpec((1, d), fixed)],
        out_specs=[pl.BlockSpec((tm, d), row), pl.BlockSpec((tm, d), row)],
        out_shape=[jax.ShapeDtypeStruct((m, d), F32), jax.ShapeDtypeStruct((m, d), BF16)],
        compiler_params=_cparams("parallel"),
        name="matmul_res_ln",
    )(*a_list, *([w] * n_in), resid, g.reshape(1, d), b.reshape(1, d))


def _swiglu_up(x_ref, wg_ref, wu_ref):
    x = x_ref[...]
    return (_silu(_dot(x, wg_ref[...])) * _dot(x, wu_ref[...])).astype(BF16)


def _ffn_kernel(x_ref, wg_ref, wu_ref, wd_ref, r_ref, g_ref, b_ref,
                of_ref, ob_ref, acc_ref, *, alpha):
    j = pl.program_id(1)

    @pl.when(j == 0)
    def _():
        acc_ref[...] = jnp.zeros_like(acc_ref)

    acc_ref[...] += _dot(_swiglu_up(x_ref, wg_ref, wu_ref), wd_ref[...])

    @pl.when(j == pl.num_programs(1) - 1)
    def _():
        out = _layer_norm(alpha * r_ref[...] + acc_ref[...], g_ref[...], b_ref[...])
        of_ref[...] = out
        ob_ref[...] = out.astype(BF16)


def _ffn(xb, wg, wu, wd, resid, g, b, alpha, *, tm=512, tf=FFN_TILE):
    m, d = xb.shape
    f = wg.shape[1]
    tm, tf = _pick(m, tm), _pick(f, tf)
    row = lambda i, j: (i, 0)
    fixed = lambda i, j: (0, 0)
    return pl.pallas_call(
        functools.partial(_ffn_kernel, alpha=alpha),
        grid=(m // tm, f // tf),
        in_specs=[pl.BlockSpec((tm, d), row),
                  pl.BlockSpec((d, tf), lambda i, j: (0, j)),
                  pl.BlockSpec((d, tf), lambda i, j: (0, j)),
                  pl.BlockSpec((tf, d), lambda i, j: (j, 0)),
                  pl.BlockSpec((tm, d), row),
                  pl.BlockSpec((1, d), fixed), pl.BlockSpec((1, d), fixed)],
        out_specs=[pl.BlockSpec((tm, d), row), pl.BlockSpec((tm, d), row)],
        out_shape=[jax.ShapeDtypeStruct((m, d), F32), jax.ShapeDtypeStruct((m, d), BF16)],
        scratch_shapes=[pltpu.VMEM((tm, d), F32)],
        compiler_params=_cparams("parallel", "arbitrary"),
        name="ffn_dense",
    )(xb, wg, wu, wd, resid, g.reshape(1, d), b.reshape(1, d))


R_E1, R_E2, R_RANK1, R_RANK2, R_G1, R_G2 = range(6)
ROW_SLABS = 16


def _route_kernel(l_ref, tril_ref, info_ref, cnt_ref, carry_ref):
    @pl.when(pl.program_id(0) == 0)
    def _():
        carry_ref[...] = jnp.zeros_like(carry_ref)

    logits = l_ref[...]
    lane = lax.broadcasted_iota(jnp.int32, logits.shape, 1)
    logits = jnp.where(lane < N_EXPERTS, logits, -jnp.inf)
    m1 = jnp.max(logits, axis=-1, keepdims=True)
    i1 = jnp.min(jnp.where(logits == m1, lane, LANES), axis=-1, keepdims=True)
    rest = jnp.where(lane == i1, -jnp.inf, logits)
    m2 = jnp.max(rest, axis=-1, keepdims=True)
    i2 = jnp.min(jnp.where(rest == m2, lane, LANES), axis=-1, keepdims=True)
    e2 = jnp.exp(m2 - m1)
    g1 = 1.0 / (1.0 + e2)
    g2 = e2 / (1.0 + e2)
    onehot = ((lane == i1) | (lane == i2)).astype(F32)
    before = _dot_hi(tril_ref[...], onehot) + carry_ref[...]
    rank1 = jnp.sum(jnp.where(lane == i1, before, 0.0), axis=-1, keepdims=True)
    rank2 = jnp.sum(jnp.where(lane == i2, before, 0.0), axis=-1, keepdims=True)
    info = jnp.zeros(logits.shape, F32)
    for ln, val in ((R_E1, i1.astype(F32)), (R_E2, i2.astype(F32)), (R_RANK1, rank1),
                    (R_RANK2, rank2), (R_G1, g1), (R_G2, g2)):
        info = jnp.where(lane == ln, val, info)
    info_ref[...] = info
    carry_ref[...] += jnp.sum(onehot, axis=0, keepdims=True)
    cnt_ref[...] = carry_ref[...]


def _route(logits, *, tm=512):
    m = logits.shape[0]
    tm = _pick(m, tm)
    r = lax.broadcasted_iota(jnp.int32, (tm, tm), 0)
    c = lax.broadcasted_iota(jnp.int32, (tm, tm), 1)
    strict = (c < r).astype(F32)
    spec = pl.BlockSpec((tm, LANES), lambda i: (i, 0))
    return pl.pallas_call(
        _route_kernel, grid=(m // tm,),
        in_specs=[spec, pl.BlockSpec((tm, tm), lambda i: (0, 0))],
        out_specs=[spec, pl.BlockSpec((1, LANES), lambda i: (0, 0))],
        out_shape=[jax.ShapeDtypeStruct((m, LANES), F32), jax.ShapeDtypeStruct((1, LANES), F32)],
        scratch_shapes=[pltpu.VMEM((1, LANES), F32)],
        compiler_params=_cparams("arbitrary"), name="route_top2",
    )(logits, strict)


def _slot_index(pos_smem, k, r, rows_per_slot):
    return pos_smem[k * rows_per_slot + lax.shift_right_logical(r, 7), r & (LANES - 1)]


def _dispatch_kernel(pos_hbm, x_ref, zero_hbm, xs_hbm, pos_smem, psem, sem, *, tm):
    del zero_hbm
    i = pl.program_id(0)
    pcopy = pltpu.make_async_copy(pos_hbm.at[i], pos_smem, psem)
    pcopy.start()
    pcopy.wait()
    rps = tm // LANES

    def row_copy(r, k):
        return pltpu.make_async_copy(x_ref.at[r], xs_hbm.at[_slot_index(pos_smem, k, r, rps)], sem)

    def issue(r, c):
        for k in range(TOP_K):
            row_copy(r, k).start()
        return c

    def drain(r, c):
        for k in range(TOP_K):
            row_copy(r, k).wait()
        return c

    lax.fori_loop(0, tm, issue, 0)
    lax.fori_loop(0, tm, drain, 0)


def _moe_dispatch(x3, pos3, n_rows, *, tm):
    t = x3.shape[0]
    zeros = jnp.zeros((n_rows,) + x3.shape[1:], x3.dtype)
    any_spec = pl.BlockSpec(memory_space=pl.ANY)
    return pl.pallas_call(
        functools.partial(_dispatch_kernel, tm=tm),
        grid=(t // tm,),
        in_specs=[any_spec, pl.BlockSpec((tm,) + x3.shape[1:], lambda i: (i, 0, 0)), any_spec],
        out_specs=any_spec,
        out_shape=jax.ShapeDtypeStruct(zeros.shape, zeros.dtype),
        scratch_shapes=[pltpu.SMEM(pos3.shape[1:], jnp.int32),
                        pltpu.SemaphoreType.DMA, pltpu.SemaphoreType.DMA],
        input_output_aliases={2: 0},
        compiler_params=_cparams("arbitrary"),
        name="moe_dispatch",
    )(pos3, x3, zeros)


def _moe_ffn_kernel(te_ref, nv_ref, x_ref, wg_ref, wu_ref, wd_ref, o_ref, acc_ref, h_ref):
    i, j = pl.program_id(0), pl.program_id(1)
    nf = pl.num_programs(1) - 1
    valid = i < nv_ref[0]

    @pl.when(valid & (j == 0))
    def _():
        acc_ref[...] = jnp.zeros_like(acc_ref)
        h_ref[...] = _swiglu_up(x_ref, wg_ref, wu_ref)

    @pl.when(valid & (j > 0) & (j < nf))
    def _():
        h_next = _swiglu_up(x_ref, wg_ref, wu_ref)
        acc_ref[...] += _dot(h_ref[...], wd_ref[...])
        h_ref[...] = h_next

    @pl.when(valid & (j == nf))
    def _():
        o_ref[...] = acc_ref[...] + _dot(h_ref[...], wd_ref[...])

    @pl.when(jnp.logical_not(valid) & (j == nf))
    def _():
        o_ref[...] = jnp.zeros_like(o_ref)


def _moe_ffn(xs, wg, wu, wd, tile_expert, n_valid, *, tm, tf=MOE_TILE):
    n_rows, d = xs.shape
    f = wg.shape[2]
    tf = _pick(f, tf)
    nf = f // tf

    def up(i, j, nv):
        return jnp.where(i < nv[0], jnp.minimum(j, nf - 1), nf - 1)

    def down(i, j, nv):
        return jnp.where(i < nv[0], jnp.maximum(j - 1, 0), nf - 1)

    grid_spec = pltpu.PrefetchScalarGridSpec(
        num_scalar_prefetch=2,
        grid=(n_rows // tm, nf + 1),
        in_specs=[pl.BlockSpec((tm, d), lambda i, j, te, nv: (i, 0)),
                  pl.BlockSpec((None, d, tf), lambda i, j, te, nv: (te[i], 0, up(i, j, nv))),
                  pl.BlockSpec((None, d, tf), lambda i, j, te, nv: (te[i], 0, up(i, j, nv))),
                  pl.BlockSpec((None, tf, d), lambda i, j, te, nv: (te[i], down(i, j, nv), 0))],
        out_specs=pl.BlockSpec((tm, d), lambda i, j, te, nv: (i, 0)),
        scratch_shapes=[pltpu.VMEM((tm, d), F32), pltpu.VMEM((tm, tf), BF16)])
    return pl.pallas_call(
        _moe_ffn_kernel, grid_spec=grid_spec,
        out_shape=jax.ShapeDtypeStruct((n_rows, d), F32),
        compiler_params=_cparams("arbitrary", "arbitrary"),
        name="moe_ffn",
    )(tile_expert, n_valid, xs, wg, wu, wd)


def _combine_kernel(pos_hbm, ys_hbm, info_ref, r_ref, g_ref, b_ref, of_ref, ob_ref,
                    pos_smem, buf_ref, z_ref, psem, sem, *, tm, alpha):
    i = pl.program_id(0)
    pcopy = pltpu.make_async_copy(pos_hbm.at[i], pos_smem, psem)
    pcopy.start()
    pcopy.wait()
    rps = tm // LANES

    def row_copy(r, k):
        dst = buf_ref.at[k, pl.ds(pl.multiple_of(r * ROW_SLABS, ROW_SLABS), ROW_SLABS)]
        return pltpu.make_async_copy(ys_hbm.at[_slot_index(pos_smem, k, r, rps)], dst, sem)

    def issue(r, c):
        for k in range(TOP_K):
            row_copy(r, k).start()
        return c

    def drain(r, c):
        for k in range(TOP_K):
            row_copy(r, k).wait()
        return c

    lax.fori_loop(0, tm, issue, 0)
    lax.fori_loop(0, tm, drain, 0)

    info = info_ref[...]
    g1 = info[:, R_G1:R_G1 + 1]
    g2 = info[:, R_G2:R_G2 + 1]
    for a in range(ROW_SLABS):
        y1 = buf_ref[0, pl.ds(a, tm, stride=ROW_SLABS), :]
        y2 = buf_ref[1, pl.ds(a, tm, stride=ROW_SLABS), :]
        z_ref[:, a * LANES:(a + 1) * LANES] = g1 * y1 + g2 * y2
    out = _layer_norm(alpha * r_ref[...] + z_ref[...], g_ref[...], b_ref[...])
    of_ref[...] = out
    ob_ref[...] = out.astype(BF16)


def _moe_combine(ys3, pos3, info, resid, g, b, alpha, *, tm):
    t, d = resid.shape
    row = lambda i: (i, 0)
    fixed = lambda i: (0, 0)
    any_spec = pl.BlockSpec(memory_space=pl.ANY)
    return pl.pallas_call(
        functools.partial(_combine_kernel, tm=tm, alpha=alpha),
        grid=(t // tm,),
        in_specs=[any_spec, any_spec, pl.BlockSpec((tm, LANES), row), pl.BlockSpec((tm, d), row),
                  pl.BlockSpec((1, d), fixed), pl.BlockSpec((1, d), fixed)],
        out_specs=[pl.BlockSpec((tm, d), row), pl.BlockSpec((tm, d), row)],
        out_shape=[jax.ShapeDtypeStruct((t, d), F32), jax.ShapeDtypeStruct((t, d), BF16)],
        scratch_shapes=[pltpu.SMEM(pos3.shape[1:], jnp.int32),
                        pltpu.VMEM((TOP_K, tm * ROW_SLABS, LANES), F32),
                        pltpu.VMEM((tm, d), F32),
                        pltpu.SemaphoreType.DMA, pltpu.SemaphoreType.DMA],
        compiler_params=_cparams("arbitrary"),
        name="moe_combine",
    )(pos3, ys3, info, resid, g.reshape(1, d), b.reshape(1, d))


def _slot_table(pos, tm):
    t = pos.shape[1]
    p = pos.reshape(TOP_K, t // tm, tm // LANES, LANES)
    return jnp.transpose(p, (1, 0, 2, 3)).reshape(t // tm, TOP_K * tm // LANES, LANES)


def _moe(xf, xb, router_w, wg, wu, wd, g, b, alpha, *, tm_rows=1024, tm_disp=512, tm_comb=256):
    t, d = xf.shape
    ne = router_w.shape[1]
    w_r = jnp.zeros((d, LANES), F32).at[:, :ne].set(router_w.astype(F32))
    logits = _matmul(xf, w_r, F32, tn=LANES, hi=True)
    info, counts = _route(logits)

    tm_rows = _pick(TOP_K * t, tm_rows)
    n_tiles = TOP_K * t // tm_rows + ne
    counts = counts[0, :ne].astype(jnp.int32)
    tiles_e = (counts + tm_rows - 1) // tm_rows
    tile_end = jnp.cumsum(tiles_e)
    row_off = (tile_end - tiles_e) * tm_rows
    tile_ids = jnp.arange(n_tiles, dtype=jnp.int32)
    tile_expert = jnp.sum(tile_ids[:, None] >= tile_end[None, :], axis=1).astype(jnp.int32)
    tile_expert = jnp.minimum(tile_expert, ne - 1)
    last_e = jnp.max(jnp.where(tiles_e > 0, jnp.arange(ne, dtype=jnp.int32), 0))
    n_valid = tile_end[-1:].astype(jnp.int32)
    tile_expert = jnp.where(tile_ids < n_valid[0], tile_expert, last_e)
    e_idx = info[:, R_E1:R_E2 + 1].astype(jnp.int32)
    rank = info[:, R_RANK1:R_RANK2 + 1].astype(jnp.int32)
    off = jnp.sum(jnp.where(e_idx[:, :, None] == jnp.arange(ne, dtype=jnp.int32),
                            row_off[None, None, :], 0), axis=-1)
    pos = (off + rank).T

    tm_disp, tm_comb = _pick(t, tm_disp), _pick(t, tm_comb)
    n_rows = n_tiles * tm_rows
    x3 = xb.reshape(t, ROW_SLABS, d // ROW_SLABS)
    xs = _moe_dispatch(x3, _slot_table(pos, tm_disp), n_rows, tm=tm_disp)
    ys = _moe_ffn(xs.reshape(n_rows, d), wg, wu, wd, tile_expert, n_valid, tm=tm_rows)
    ys3 = ys.reshape(n_rows, ROW_SLABS, d // ROW_SLABS)
    return _moe_combine(ys3, _slot_table(pos, tm_comb), info, xf, g, b, alpha, tm=tm_comb)


def _xattn_kernel(x_ref, wq_ref, k_ref, v_ref, o_ref):
    d = wq_ref.shape[1]
    hd = d // XA_HEADS
    q = _dot(x_ref[...], wq_ref[...]).astype(BF16)
    for h in range(XA_HEADS):
        sl = slice(h * hd, (h + 1) * hd)
        s = _dot_nt(q[:, sl], k_ref[:, sl]) * (hd ** -0.5)
        s = s - jnp.max(s, axis=-1, keepdims=True)
        p = jnp.exp(s)
        p = p / jnp.sum(p, axis=-1, keepdims=True)
        o_ref[:, sl] = _dot(p.astype(BF16), v_ref[:, sl]).astype(o_ref.dtype)


def _xattn(xb, wq, km, vm, nb, *, tm=256):
    t, d = xb.shape
    s = t // nb
    ml = km.shape[1]
    tm = _pick(s, tm)
    ns = s // tm
    return pl.pallas_call(
        _xattn_kernel,
        grid=(nb, ns),
        in_specs=[pl.BlockSpec((tm, d), lambda b, i: (b * ns + i, 0)),
                  pl.BlockSpec((d, d), lambda b, i: (0, 0)),
                  pl.BlockSpec((None, ml, d), lambda b, i: (b, 0, 0)),
                  pl.BlockSpec((None, ml, d), lambda b, i: (b, 0, 0))],
        out_specs=pl.BlockSpec((tm, d), lambda b, i: (b * ns + i, 0)),
        out_shape=jax.ShapeDtypeStruct((t, d), BF16),
        compiler_params=_cparams("parallel", "arbitrary"),
        name="xattn",
    )(xb, wq, km, vm)


def _fox_cum_kernel(gt_ref, bias_ref, c_ref, carry_ref):
    @pl.when(pl.program_id(1) == 0)
    def _():
        carry_ref[...] = jnp.zeros_like(carry_ref)

    z = gt_ref[...] + bias_ref[...]
    lf = -_softplus(-z)
    tk = z.shape[1]
    r = lax.broadcasted_iota(jnp.int32, (tk, tk), 0)
    c = lax.broadcasted_iota(jnp.int32, (tk, tk), 1)
    upper = (r <= c).astype(F32)
    cum = _dot_hi(lf, upper) + carry_ref[...]
    c_ref[...] = cum
    carry_ref[...] = cum[:, tk - 1:tk]


def _fox_cum(gates_t, bias, nb, *, tk=512):
    t = gates_t.shape[1]
    s = t // nb
    tk = _pick(s, tk)
    ns = s // tk
    rb = GATE_FOX // SUBLANES
    return pl.pallas_call(
        _fox_cum_kernel,
        grid=(nb, ns),
        in_specs=[pl.BlockSpec((SUBLANES, tk), lambda b, i: (rb, b * ns + i)),
                  pl.BlockSpec((SUBLANES, 1), lambda b, i: (0, 0))],
        out_specs=pl.BlockSpec((None, SUBLANES, tk), lambda b, i: (b, 0, i)),
        out_shape=jax.ShapeDtypeStruct((nb, SUBLANES, s), F32),
        scratch_shapes=[pltpu.VMEM((SUBLANES, 1), F32)],
        compiler_params=_cparams("parallel", "arbitrary"),
        name="fox_cumsum",
    )(gates_t, bias)


def _fox_kernel(q_ref, k_ref, v_ref, c_ref, o_ref, *, tq, tk, scale):
    assert tq == tk
    qi = pl.program_id(2)
    q = q_ref[...]
    q0 = pl.multiple_of(qi * tq, tq)
    c_q0 = c_ref[:, pl.ds(q0, LANES)][:, 0:1]

    def step(kj, carry, diagonal):
        m, l, acc = carry
        k0 = pl.multiple_of(kj * tk, tk)
        kb = k_ref[pl.ds(k0, tk), :]
        vb = v_ref[pl.ds(k0, tk), :]
        s = _dot_nt(q, kb) * scale + (c_q0 - c_ref[:, pl.ds(k0, tk)])
        if diagonal:
            row = lax.broadcasted_iota(jnp.int32, (tq, tk), 0)
            col = lax.broadcasted_iota(jnp.int32, (tq, tk), 1)
            s = jnp.where(col <= row, s, MASK_VALUE)
        m_new = jnp.maximum(m, jnp.max(s, axis=-1, keepdims=True))
        a = jnp.exp(m - m_new)
        p = jnp.exp(s - m_new)
        l = a * l + jnp.sum(p, axis=-1, keepdims=True)
        acc = a * acc + _dot(p.astype(BF16), vb)
        return m_new, l, acc

    init = (jnp.full((tq, 1), MASK_VALUE, F32), jnp.zeros((tq, 1), F32),
            jnp.zeros((tq, HEAD_DIM), F32))
    carry = lax.fori_loop(0, qi, functools.partial(step, diagonal=False), init)
    m, l, acc = step(qi, carry, True)
    o_ref[...] = (acc / l).astype(o_ref.dtype)


def _fox_attention(proj, cum, nb, nh, *, tq=512, tk=512):
    t = proj.shape[0]
    s = t // nb
    tq, tk = _pick(s, tq), _pick(s, tk)
    nq = s // tq
    proj3 = proj.reshape(nb, s, proj.shape[1])
    cum2 = cum.reshape(nb * SUBLANES, 1, s)
    out = pl.pallas_call(
        functools.partial(_fox_kernel, tq=tq, tk=tk, scale=HEAD_DIM ** -0.5),
        grid=(nb, nh, nq),
        in_specs=[pl.BlockSpec((None, tq, HEAD_DIM), lambda b, h, i: (b, i, h)),
                  pl.BlockSpec((None, s, HEAD_DIM), lambda b, h, i: (b, 0, nh + h)),
                  pl.BlockSpec((None, s, HEAD_DIM), lambda b, h, i: (b, 0, 2 * nh + h)),
                  pl.BlockSpec((None, 1, s), lambda b, h, i: (b * SUBLANES + h, 0, 0))],
        out_specs=pl.BlockSpec((None, tq, HEAD_DIM), lambda b, h, i: (b, i, h)),
        out_shape=jax.ShapeDtypeStruct((nb, s, nh * HEAD_DIM), BF16),
        compiler_params=_cparams("parallel", "parallel", "arbitrary"),
        name="fox_attention",
    )(proj3, proj3, proj3, cum2)
    return out.reshape(t, nh * HEAD_DIM)


HGRN_SUB = 16


def _hgrn_kernel(p_ref, lb_ref, ng_ref, tri_ref, o_ref, st_ref, *, nh, c):
    @pl.when(pl.program_id(1) == 0)
    def _():
        st_ref[...] = jnp.zeros_like(st_ref)

    w = nh * HEAD_DIM
    row = lax.broadcasted_iota(jnp.int32, (c, c), 0)
    col = lax.broadcasted_iota(jnp.int32, (c, c), 1)
    tri = tri_ref[...]
    heads = range(nh)
    sl = lambda grp, h: slice(grp * w + h * HEAD_DIM, grp * w + (h + 1) * HEAD_DIM)
    q = [_silu(p_ref[:, sl(0, h)].astype(F32)) for h in heads]
    f = [lb_ref[:, sl(0, h)] + (1.0 - lb_ref[:, sl(0, h)]) * _sigmoid(p_ref[:, sl(1, h)].astype(F32))
         for h in heads]
    k = [1.0 - f[h] for h in heads]
    vb = [p_ref[:, sl(2, h)] for h in heads]
    bc = [_dot_hi(tri, jnp.log(f[h])) for h in heads]

    def scaled(x, b3, shift):
        return (x.reshape(b3.shape) * jnp.exp(shift)).reshape(c, HEAD_DIM).astype(BF16)

    nsub = c // HGRN_SUB
    same = (row // HGRN_SUB == col // HGRN_SUB) & (col <= row)
    a = []
    for h in heads:
        b3 = bc[h].reshape(nsub, HGRN_SUB, HEAD_DIM)
        ref = b3[:, HGRN_SUB // 2 - 1:HGRN_SUB // 2, :]
        a.append(jnp.where(same, _dot_nt(scaled(q[h], b3, b3 - ref), scaled(k[h], b3, ref - b3)), 0.0))
    s = HGRN_SUB
    while s < c:
        rb, cb = row // s, col // s
        pair = (rb % 2 == 1) & (cb == rb - 1)
        for h in heads:
            b3 = bc[h].reshape(c // s, s, HEAD_DIM)
            last = b3[:, s - 1:s, :]
            prev = jnp.concatenate([jnp.zeros_like(last[:1]), last[:-1]], axis=0)
            a[h] = a[h] + jnp.where(
                pair, _dot_nt(scaled(q[h], b3, b3 - prev), scaled(k[h], b3, last - b3)), 0.0)
        s *= 2

    st = [st_ref[h] for h in heads]
    o = [_dot(a[h].astype(BF16), vb[h])
         + _dot_nt((q[h] * jnp.exp(bc[h])).astype(BF16), st[h].astype(BF16)) for h in heads]
    for h in heads:
        b_last = bc[h][c - 1:c, :]
        k_st = k[h] * jnp.exp(b_last - bc[h])
        st_ref[h] = (st[h] * jnp.exp(b_last)
                     + _dot(vb[h].astype(F32).T.astype(BF16), k_st.astype(BF16)))
    for h in heads:
        gate = _sigmoid(p_ref[:, sl(3, h)].astype(F32))
        oh = o[h] * lax.rsqrt(jnp.mean(o[h] * o[h], axis=-1, keepdims=True) + RMS_EPS) * ng_ref[...]
        o_ref[:, h * HEAD_DIM:(h + 1) * HEAD_DIM] = (oh * gate).astype(o_ref.dtype)


def _tri_lower(c):
    r = lax.broadcasted_iota(jnp.int32, (c, c), 0)
    col = lax.broadcasted_iota(jnp.int32, (c, c), 1)
    return (col <= r).astype(F32)


def _hgrn(proj, lb, norm_g, nb, nh, *, c=128):
    t, wtot = proj.shape
    s = t // nb
    c = _pick(s, c)
    ns = s // c
    w = nh * HEAD_DIM
    fixed = lambda b, i: (0, 0)
    return pl.pallas_call(
        functools.partial(_hgrn_kernel, nh=nh, c=c),
        grid=(nb, ns),
        in_specs=[pl.BlockSpec((c, wtot), lambda b, i: (b * ns + i, 0)),
                  pl.BlockSpec((1, w), fixed), pl.BlockSpec((1, HEAD_DIM), fixed),
                  pl.BlockSpec((c, c), fixed)],
        out_specs=pl.BlockSpec((c, w), lambda b, i: (b * ns + i, 0)),
        out_shape=jax.ShapeDtypeStruct((t, w), BF16),
        scratch_shapes=[pltpu.VMEM((nh, HEAD_DIM, HEAD_DIM), F32)],
        compiler_params=_cparams("parallel", "arbitrary"),
        name="hgrn2",
    )(proj, lb, norm_g.reshape(1, HEAD_DIM), _tri_lower(c))


def _level_masks(c):
    r = lax.broadcasted_iota(jnp.int32, (c, c), 0)
    col = lax.broadcasted_iota(jnp.int32, (c, c), 1)
    out, s = [], 1
    while s < c:
        rb, cb = r // s, col // s
        out.append(((rb % 2 == 1) & (cb == rb - 1)).astype(F32))
        s *= 2
    return jnp.stack(out)


def _gdn_kernel(p_ref, halo_ref, gt_ref, gtt_ref, cw_ref, av_ref, dtv_ref, avt_ref, dtt_ref,
                ng_ref, tri_ref, lvl_ref, o_ref, s_ref, *, nh, c):
    first = pl.program_id(1) == 0

    @pl.when(first)
    def _():
        s_ref[...] = jnp.zeros_like(s_ref)

    w = nh * HEAD_DIM
    row = lax.broadcasted_iota(jnp.int32, (c, c), 0)
    col = lax.broadcasted_iota(jnp.int32, (c, c), 1)
    causal = col <= row
    tri = tri_ref[...]
    gates = gt_ref[...]
    la = -av_ref[...] * _softplus(gates + dtv_ref[...])
    g_col = _dot_hi(tri, la)
    la_t = -avt_ref[...] * _softplus(gtt_ref[...] + dtt_ref[...])
    g_row = lax.dot_general(la_t, tri, (((1,), (1,)), ((), ())),
                            preferred_element_type=F32, precision=HI)
    halo_scale = jnp.where(first, 0.0, 1.0)
    sub = lax.broadcasted_iota(jnp.int32, (SUBLANES, HEAD_DIM), 0)

    def conv_silu(grp, h):
        lo = grp * w + h * HEAD_DIM
        cur = p_ref[:, lo:lo + HEAD_DIM].astype(F32)
        halo = halo_ref[:, lo:lo + HEAD_DIM].astype(F32) * halo_scale
        cw = cw_ref[:, lo:lo + HEAD_DIM]
        y = cw[CONV_K - 1:CONV_K, :] * cur
        for j in range(1, CONV_K):
            rolled = pltpu.roll(cur, j, axis=0)
            head = jnp.where(sub < j, pltpu.roll(halo, j, axis=0), rolled[:SUBLANES])
            shifted = jnp.concatenate([head, rolled[SUBLANES:]], axis=0)
            y = y + cw[CONV_K - 1 - j:CONV_K - j, :] * shifted
        return _silu(y)

    def l2n(x):
        return x * lax.rsqrt(jnp.sum(x * x, axis=-1, keepdims=True) + L2_EPS)

    heads = range(nh)
    q = [l2n(conv_silu(0, h)) * (HEAD_DIM ** -0.5) for h in heads]
    k = [l2n(conv_silu(1, h)) for h in heads]
    v = [conv_silu(2, h) for h in heads]
    beta = [_sigmoid(gates[:, GATE_BETA + h:GATE_BETA + h + 1]) for h in heads]
    gc = [g_col[:, GATE_DEC + h:GATE_DEC + h + 1] for h in heads]
    gamma = [jnp.where(causal, jnp.exp(jnp.where(causal, gc[h] - g_row[h:h + 1, :], 0.0)), 0.0)
             for h in heads]
    kb = [k[h] * beta[h] for h in heads]
    kbf = [k[h].astype(BF16) for h in heads]
    m = [_dot_nt(kb[h].astype(BF16), kbf[h]) * gamma[h] for h in heads]

    n = [-(m[h] * lvl_ref[0]) for h in heads]
    for lv in range(1, lvl_ref.shape[0]):
        low = [m[h] * lvl_ref[lv] for h in heads]
        p = [low[h] + _dot(n[h].astype(BF16), low[h].astype(BF16)) for h in heads]
        n = [n[h] - p[h] - _dot(p[h].astype(BF16), n[h].astype(BF16)) for h in heads]

    eg = [jnp.exp(gc[h]) for h in heads]
    g_last = [gc[h][c - 1:c, :] for h in heads]
    rhs = [jnp.concatenate([v[h] * beta[h], kb[h] * eg[h]], axis=1) for h in heads]
    sol = [rhs[h] + _dot(n[h].astype(BF16), rhs[h].astype(BF16)) for h in heads]
    state = [s_ref[h] for h in heads]
    sb = [state[h].astype(BF16) for h in heads]
    v_new = [sol[h][:, :HEAD_DIM] - _dot(sol[h][:, HEAD_DIM:].astype(BF16), sb[h]) for h in heads]
    qk = [_dot_nt(q[h].astype(BF16), kbf[h]) * gamma[h] for h in heads]
    o = [_dot((q[h] * eg[h]).astype(BF16), sb[h]) + _dot(qk[h].astype(BF16), v_new[h].astype(BF16))
         for h in heads]
    k_dec = [k[h] * jnp.exp(g_last[h] - gc[h]) for h in heads]
    for h in heads:
        s_ref[h] = (state[h] * jnp.exp(g_last[h])
                    + _dot(k_dec[h].T.astype(BF16), v_new[h].astype(BF16)))
    for h in heads:
        z = p_ref[:, 3 * w + h * HEAD_DIM:3 * w + (h + 1) * HEAD_DIM].astype(F32)
        oh = o[h] * lax.rsqrt(jnp.mean(o[h] * o[h], axis=-1, keepdims=True) + RMS_EPS) * ng_ref[...]
        o_ref[:, h * HEAD_DIM:(h + 1) * HEAD_DIM] = (oh * _silu(z)).astype(o_ref.dtype)


def _gdn(proj, gates, gates_t, conv_w, a_log, dt_bias, norm_g, nb, nh, *, c=128):
    t, wtot = proj.shape
    s = t // nb
    c = _pick(s, c)
    ns = s // c
    w = nh * HEAD_DIM
    a = jnp.exp(a_log.astype(F32))
    av = jnp.zeros((1, LANES), F32).at[0, GATE_DEC:GATE_DEC + nh].set(a)
    dtv = jnp.zeros((1, LANES), F32).at[0, GATE_DEC:GATE_DEC + nh].set(dt_bias.astype(F32))
    avt = jnp.zeros((SUBLANES, 1), F32).at[:nh, 0].set(a)
    dtt = jnp.zeros((SUBLANES, 1), F32).at[:nh, 0].set(dt_bias.astype(F32))
    lvl = _level_masks(c)
    fixed = lambda b, i: (0, 0)
    rows_per_halo = c // SUBLANES
    return pl.pallas_call(
        functools.partial(_gdn_kernel, nh=nh, c=c),
        grid=(nb, ns),
        in_specs=[pl.BlockSpec((c, wtot), lambda b, i: (b * ns + i, 0)),
                  pl.BlockSpec((SUBLANES, 3 * w),
                               lambda b, i: (jnp.maximum((b * ns + i) * rows_per_halo - 1, 0), 0)),
                  pl.BlockSpec((c, LANES), lambda b, i: (b * ns + i, 0)),
                  pl.BlockSpec((SUBLANES, c), lambda b, i: (GATE_DEC // SUBLANES, b * ns + i)),
                  pl.BlockSpec((CONV_K, 3 * w), fixed),
                  pl.BlockSpec((1, LANES), fixed), pl.BlockSpec((1, LANES), fixed),
                  pl.BlockSpec((SUBLANES, 1), fixed), pl.BlockSpec((SUBLANES, 1), fixed),
                  pl.BlockSpec((1, HEAD_DIM), fixed),
                  pl.BlockSpec((c, c), fixed),
                  pl.BlockSpec(lvl.shape, lambda b, i: (0, 0, 0))],
        out_specs=pl.BlockSpec((c, w), lambda b, i: (b * ns + i, 0)),
        out_shape=jax.ShapeDtypeStruct((t, w), BF16),
        scratch_shapes=[pltpu.VMEM((nh, HEAD_DIM, HEAD_DIM), F32)],
        compiler_params=_cparams("parallel", "arbitrary"),
        name="gated_delta",
    )(proj, proj, gates, gates_t, conv_w.astype(F32), av, dtv, avt, dtt,
      norm_g.reshape(1, HEAD_DIM), _tri_lower(c), lvl)


def kernel(x, mem, w_in, conv_w, gdn_a_log, gdn_dt_bias, gdn_norm_g, fox_f_bias,
           hgrn_lb_logits, hgrn_norm_g, w_out, xa_wq, xa_wk, xa_wv, xa_wo, ln_g, ln_b,
           ffn_wg, ffn_wu, ffn_wd, router_w, moe_wg, moe_wu, moe_wd):
    nb, s, d = x.shape
    depth = w_in.shape[0]
    t = nb * s
    n_mix = d // HEAD_DIM
    hgrn_h = n_mix // 4
    gdn_h = (n_mix - hgrn_h) // 2
    fox_h = n_mix - hgrn_h - gdn_h
    gw, fw, hw = gdn_h * HEAD_DIM, fox_h * HEAD_DIM, hgrn_h * HEAD_DIM
    alpha = (2.0 * depth) ** 0.25
    ne = router_w.shape[-1]

    o_beta = 4 * gw
    o_dec = o_beta + gdn_h
    o_fox = o_dec + gdn_h
    o_fb = o_fox + 3 * fw
    o_hgrn = o_fb + fox_h

    lb_w = jax.nn.softmax(hgrn_lb_logits.astype(F32), axis=0)
    lower_bounds = jnp.cumsum(lb_w, axis=0) - lb_w[:1]

    xf = x.reshape(t, d).astype(F32)
    xb = xf.astype(BF16)
    mem_b = mem.reshape(nb * mem.shape[1], d).astype(BF16)

    for l in range(depth):
        wl = w_in[l]
        w_g = wl[:, :o_beta].astype(BF16)
        w_f = wl[:, o_fox:o_fb].astype(BF16)
        w_h = wl[:, o_hgrn:].astype(BF16)
        w_gate = jnp.zeros((d, LANES), F32)
        w_gate = w_gate.at[:, GATE_BETA:GATE_BETA + gdn_h].set(wl[:, o_beta:o_dec])
        w_gate = w_gate.at[:, GATE_DEC:GATE_DEC + gdn_h].set(wl[:, o_dec:o_fox])
        w_gate = w_gate.at[:, GATE_FOX:GATE_FOX + fox_h].set(wl[:, o_fb:o_hgrn])

        pg = _matmul(xb, w_g, BF16)
        pf = _matmul(xb, w_f, BF16)
        ph = _matmul(xb, w_h, BF16)
        gates = _matmul(xf, w_gate, F32, tn=LANES, hi=True)
        gates_t = gates.T

        o_a = _gdn(pg, gates, gates_t, conv_w[l], gdn_a_log[l], gdn_dt_bias[l], gdn_norm_g[l],
                   nb, gdn_h)
        fbias = jnp.zeros((SUBLANES, 1), F32).at[:fox_h, 0].set(fox_f_bias[l].astype(F32))
        cum = _fox_cum(gates_t, fbias, nb)
        o_b = _fox_attention(pf, cum, nb, fox_h)
        o_c = _hgrn(ph, lower_bounds[l].reshape(1, hw), hgrn_norm_g[l], nb, hgrn_h)

        blocks = [(gw, 0), (fw, gw // fw), (hw, (gw + fw) // hw)]
        xf, xb = _matmul_res_ln([o_a, o_b, o_c], w_out[l].astype(BF16), blocks, xf,
                                ln_g[l, 0], ln_b[l, 0], alpha)

        km = _matmul(mem_b, xa_wk[l].astype(BF16), BF16).reshape(nb, -1, d)
        vm = _matmul(mem_b, xa_wv[l].astype(BF16), BF16).reshape(nb, -1, d)
        xo = _xattn(xb, xa_wq[l].astype(BF16), km, vm, nb)
        xf, xb = _matmul_res_ln([xo], xa_wo[l].astype(BF16), [(d, 0)], xf,
                                ln_g[l, 1], ln_b[l, 1], alpha)

        i = l // 2
        if l % 2 == 0:
            xf, xb = _ffn(xb, ffn_wg[i].astype(BF16), ffn_wu[i].astype(BF16),
                          ffn_wd[i].astype(BF16), xf, ln_g[l, 2], ln_b[l, 2], alpha)
        else:
            xf, xb = _moe(xf, xb, router_w[i], moe_wg[i].astype(BF16), moe_wu[i].astype(BF16),
                          moe_wd[i].astype(BF16), ln_g[l, 2], ln_b[l, 2], alpha)
    return xf.reshape(nb, s, d).astype(x.dtype)
```
